```python
import jax, jax.numpy as jnp
from jax import lax
import numpy as np

D_MODEL = 2048
BATCH = 8
SEQ = 4096
DEPTH = 1

GRID_W = 64
CTX_LEN = 256
D_FF = 5632
MIX_WIDTH = D_MODEL
MLSTM_HEADS = 4
MLSTM_DV = MIX_WIDTH // 2 // MLSTM_HEADS
MLSTM_DK = MLSTM_DV // 2
MLSTM_WIDTH = MLSTM_HEADS * MLSTM_DV
QK_WIDTH = 2 * MLSTM_HEADS * MLSTM_DK
CONV_K = 5
MLSTM_CHUNK = 128
GMLP_CHUNK = 128
GMLP_GROUPS = 8
GMLP_WIDTH = MIX_WIDTH - MLSTM_WIDTH
GMLP_GD = GMLP_WIDTH // GMLP_GROUPS
CHUNK_ROWS = GMLP_CHUNK // GRID_W
N_MOD = 9
EPS = 1e-6
COL_QK_END = QK_WIDTH
COL_V_END = COL_QK_END + MLSTM_WIDTH
COL_O_END = COL_V_END + MLSTM_WIDTH
COL_GATE_END = COL_O_END + 4 * MLSTM_HEADS
IN_COLS = COL_GATE_END + 2 * GMLP_WIDTH

kernel_name = "hymba_mlstm_gmlp_macaron_dit"


def rms_norm(x, g):
    xf = x.astype(jnp.float32)
    y = xf * lax.rsqrt(jnp.mean(xf * xf, axis=-1, keepdims=True) + EPS)
    return (y * g.astype(jnp.float32)).astype(x.dtype)


def modulate(x, shift, scale):
    return x * (1 + scale) + shift


def adaln(cond, w, b):
    mod = jax.nn.silu(cond) @ w + b
    return jnp.split(mod, N_MOD, axis=-1)


def swiglu(x, w_in, w_out):
    gate, up = jnp.split(x @ w_in, 2, axis=-1)
    return (jax.nn.silu(gate) * up) @ w_out


def centred_conv(x, w, b):
    T = x.shape[1]
    p = w.shape[0] // 2
    xp = jnp.pad(x, ((0, 0), (p, p), (0, 0)))
    y = b
    for j in range(w.shape[0]):
        y = y + xp[:, j:j + T] * w[j]
    return y


def _to_chunks(a, n_chunks):
    B, T, H, d = a.shape
    return a.astype(jnp.float32).reshape(B, n_chunks, T // n_chunks, H, d).transpose(1, 0, 3, 2, 4)


def _gate_chunks(a, n_chunks):
    B, T, H = a.shape
    return a.astype(jnp.float32).reshape(B, n_chunks, T // n_chunks, H).transpose(1, 0, 3, 2)


def _state_update(state, k, v, logi, b):
    C, n, m = state
    b_last = b[..., -1]
    g = b_last[..., None] - b + logi
    m_new = jnp.maximum(b_last + m, jnp.max(g, axis=-1))
    decay = jnp.exp(b_last + m - m_new)
    kw = k * jnp.exp(g - m_new[..., None])[..., None]
    C_new = decay[..., None, None] * C + jnp.einsum('bhld,bhle->bhde', kw, v)
    n_new = decay[..., None] * n + jnp.sum(kw, axis=2)
    return (C_new, n_new, m_new)


def _chunk_output(state, q, k, v, logi, b):
    C, n, m = state
    L = q.shape[2]
    tri = jnp.tril(jnp.ones((L, L), dtype=bool))
    d_mat = jnp.where(tri, b[..., :, None] - b[..., None, :] + logi[..., None, :], -jnp.inf)
    m_inter = b + m[..., None]
    m_tok = jnp.maximum(m_inter, jnp.max(d_mat, axis=-1))
    s = jnp.einsum('bhjd,bhld->bhjl', q, k) * jnp.exp(d_mat - m_tok[..., None])
    a = jnp.exp(m_inter - m_tok)
    num = a[..., None] * jnp.einsum('bhjd,bhde->bhje', q, C) + jnp.einsum('bhjl,bhle->bhje', s, v)
    nq = a * jnp.einsum('bhjd,bhd->bhj', q, n) + jnp.sum(s, axis=-1)
    return num / jnp.maximum(jnp.abs(nq), jnp.exp(-m_tok))[..., None]


def mlstm_scan(q, k, v, logi, logf, state):
    B, T, H, _ = q.shape
    nc = T // MLSTM_CHUNK
    xs = (_to_chunks(q, nc), _to_chunks(k, nc), _to_chunks(v, nc),
          _gate_chunks(logi, nc), _gate_chunks(logf, nc))

    def body(carry, inp):
        qc, kc, vc, ic, fc = inp
        b = jnp.cumsum(fc, axis=-1)
        h = _chunk_output(carry, qc, kc, vc, ic, b)
        return _state_update(carry, kc, vc, ic, b), h

    state, h = lax.scan(body, state, xs)
    h = h.transpose(1, 0, 3, 2, 4).reshape(B, T, H, -1)
    return h.astype(v.dtype), state


def mlstm_final_state(k, v, logi, logf, state):
    nc = k.shape[1] // MLSTM_CHUNK
    xs = (_to_chunks(k, nc), _to_chunks(v, nc), _gate_chunks(logi, nc), _gate_chunks(logf, nc))

    def body(carry, inp):
        kc, vc, ic, fc = inp
        return _state_update(carry, kc, vc, ic, jnp.cumsum(fc, axis=-1)), None

    state, _ = lax.scan(body, state, xs)
    return state


def mlstm_inputs(z, conv_w, conv_b, b_igate, b_fgate):
    B, T, _ = z.shape
    qk = jax.nn.silu(centred_conv(z[..., :COL_QK_END], conv_w, conv_b))
    q = qk[..., :QK_WIDTH // 2].reshape(B, T, MLSTM_HEADS, MLSTM_DK) * (MLSTM_DK ** -0.5)
    k = qk[..., QK_WIDTH // 2:].reshape(B, T, MLSTM_HEADS, MLSTM_DK)
    v = z[..., COL_QK_END:COL_V_END].reshape(B, T, MLSTM_HEADS, MLSTM_DV)
    o = jax.nn.sigmoid(z[..., COL_V_END:COL_O_END])
    gates = z[..., COL_O_END:COL_GATE_END].reshape(B, T, 2, 2, MLSTM_HEADS)
    logi = (gates[..., 0, :] + b_igate).astype(jnp.float32)
    logf = jax.nn.log_sigmoid((gates[..., 1, :] + b_fgate).astype(jnp.float32))
    return q, k, v, o, logi, logf


def mlstm_group(zc, zx, conv_w, conv_b, b_igate, b_fgate, norm_g, ctx_out):
    qc, kc, vc, oc, ic, fc = mlstm_inputs(zc, conv_w, conv_b, b_igate, b_fgate)
    qx, kx, vx, ox, ix, fx = mlstm_inputs(zx, conv_w, conv_b, b_igate, b_fgate)
    B = qx.shape[0]
    zero = (jnp.zeros((B, MLSTM_HEADS, MLSTM_DK, MLSTM_DV), jnp.float32),
            jnp.zeros((B, MLSTM_HEADS, MLSTM_DK), jnp.float32),
            jnp.zeros((B, MLSTM_HEADS), jnp.float32))
    hx = jnp.zeros_like(vx)
    hc = jnp.zeros_like(vc)
    for d in range(2):
        rev = (lambda a: jnp.flip(a, axis=1)) if d == 1 else (lambda a: a)
        if ctx_out:
            h, st = mlstm_scan(rev(qc), rev(kc), rev(vc), rev(ic[:, :, d]), rev(fc[:, :, d]), zero)
            hc = hc + rev(h)
        else:
            st = mlstm_final_state(rev(kc), rev(vc), rev(ic[:, :, d]), rev(fc[:, :, d]), zero)
        h, _ = mlstm_scan(rev(qx), rev(kx), rev(vx), rev(ix[:, :, d]), rev(fx[:, :, d]), st)
        hx = hx + rev(h)

    def finish(h, o):
        hn = rms_norm(h, norm_g.reshape(MLSTM_HEADS, MLSTM_DV))
        return hn.reshape(o.shape) * o

    return finish(hx, ox), (finish(hc, oc) if ctx_out else None)


def gmlp_group(z, w_s, b_s, norm_g, n_chunks):
    B, T, _ = z.shape
    u, v = jnp.split(jax.nn.gelu(z), 2, axis=-1)
    v = rms_norm(v, norm_g).reshape(B, n_chunks, T // n_chunks, GMLP_GROUPS, GMLP_GD)
    s = jnp.einsum('gpq,bcqgd->bcpgd', w_s, v) + b_s.T[:, :, None]
    return u * s.reshape(B, T, GMLP_WIDTH)


def setup_inputs(seed: int = 0) -> dict:
    key = jax.random.key(seed)
    ks = jax.random.split(key, 24)

    def nrm(k, shape, scale):
        return jax.random.normal(k, shape, jnp.float32) * scale

    L, D = DEPTH, D_MODEL
    return {
        "x": nrm(ks[0], (BATCH, SEQ, D), 1.0),
        "c": nrm(ks[1], (BATCH, D), 1.0),
        "ctx": nrm(ks[2], (BATCH, CTX_LEN, D), 1.0),
        "c_ctx": nrm(ks[3], (D,), 1.0),
        "w_ada": nrm(ks[4], (L, D, N_MOD * D), 0.5 * D ** -0.5),
        "b_ada": nrm(ks[5], (L, N_MOD * D), 0.02),
        "norm_ffn1": 1.0 + nrm(ks[6], (L, D), 0.02),
        "w_ffn1_in": nrm(ks[7], (L, D, 2 * D_FF), D ** -0.5),
        "w_ffn1_out": nrm(ks[8], (L, D_FF, D), D_FF ** -0.5),
        "norm_mix": 1.0 + nrm(ks[9], (L, D), 0.02),
        "w_in": nrm(ks[10], (L, D, IN_COLS), D ** -0.5),
        "conv_w": nrm(ks[11], (L, CONV_K, QK_WIDTH), CONV_K ** -0.5),
        "conv_b": nrm(ks[12], (L, QK_WIDTH), 0.02),
        "b_igate": nrm(ks[13], (L, 2, MLSTM_HEADS), 0.1),
        "b_fgate": 3.0 + 3.0 * jax.random.uniform(ks[14], (L, 2, MLSTM_HEADS), jnp.float32),
        "mlstm_norm": 1.0 + nrm(ks[15], (L, MLSTM_WIDTH), 0.02),
        "gmlp_norm": 1.0 + nrm(ks[16], (L, GMLP_WIDTH), 0.02),
        "gmlp_w": nrm(ks[17], (L, GMLP_GROUPS, GMLP_CHUNK, GMLP_CHUNK), GMLP_CHUNK ** -0.5),
        "gmlp_b": 1.0 + nrm(ks[18], (L, GMLP_GROUPS, GMLP_CHUNK), 0.1),
        "w_out": nrm(ks[19], (L, MIX_WIDTH, D), MIX_WIDTH ** -0.5),
        "norm_ffn2": 1.0 + nrm(ks[20], (L, D), 0.02),
        "w_ffn2_in": nrm(ks[21], (L, D, 2 * D_FF), D ** -0.5),
        "w_ffn2_out": nrm(ks[22], (L, D_FF, D), D_FF ** -0.5),
        "final_norm": 1.0 + nrm(ks[23], (D,), 0.02),
    }


def reference(x, c, ctx, c_ctx, w_ada, b_ada, norm_ffn1, w_ffn1_in, w_ffn1_out, norm_mix, w_in,
              conv_w, conv_b, b_igate, b_fgate, mlstm_norm, gmlp_norm, gmlp_w, gmlp_b, w_out,
              norm_ffn2, w_ffn2_in, w_ffn2_out, final_norm):
    rows = x.shape[1] // GRID_W
    n_lat_chunks = rows // CHUNK_ROWS
    n_ctx_chunks = ctx.shape[1] // GMLP_CHUNK
    for l in range(DEPTH):
        last = l == DEPTH - 1
        mx = [m[:, None, :] for m in adaln(c, w_ada[l], b_ada[l])]
        mc = adaln(c_ctx, w_ada[l], b_ada[l])

        x = x + 0.5 * mx[2] * swiglu(modulate(rms_norm(x, norm_ffn1[l]), mx[0], mx[1]), w_ffn1_in[l], w_ffn1_out[l])
        ctx = ctx + 0.5 * mc[2] * swiglu(modulate(rms_norm(ctx, norm_ffn1[l]), mc[0], mc[1]), w_ffn1_in[l], w_ffn1_out[l])

        zx = modulate(rms_norm(x, norm_mix[l]), mx[3], mx[4]) @ w_in[l]
        w_in_ctx = w_in[l][:, :COL_GATE_END] if last else w_in[l]
        zc = modulate(rms_norm(ctx, norm_mix[l]), mc[3], mc[4]) @ w_in_ctx
        hx, hc = mlstm_group(zc[..., :COL_GATE_END], zx[..., :COL_GATE_END], conv_w[l], conv_b[l],
                             b_igate[l], b_fgate[l], mlstm_norm[l], not last)
        gx = gmlp_group(zx[..., COL_GATE_END:], gmlp_w[l], gmlp_b[l], gmlp_norm[l], n_lat_chunks)
        x = x + mx[5] * (jnp.concatenate([hx, gx], axis=-1) @ w_out[l])

        x = x + 0.5 * mx[8] * swiglu(modulate(rms_norm(x, norm_ffn2[l]), mx[6], mx[7]), w_ffn2_in[l], w_ffn2_out[l])

        if not last:
            gc = gmlp_group(zc[..., COL_GATE_END:], gmlp_w[l], gmlp_b[l], gmlp_norm[l], n_ctx_chunks)
            ctx = ctx + mc[5] * (jnp.concatenate([hc, gc], axis=-1) @ w_out[l])
            ctx = ctx + 0.5 * mc[8] * swiglu(modulate(rms_norm(ctx, norm_ffn2[l]), mc[6], mc[7]), w_ffn2_in[l], w_ffn2_out[l])
    return rms_norm(x, final_norm)
```

```python
import functools

import jax
import jax.numpy as jnp
from jax import lax
from jax.experimental import pallas as pl
from jax.experimental.pallas import tpu as pltpu

F32 = jnp.float32
BF16 = jnp.bfloat16

EPS = 1e-6
N_MOD = 9
HEADS = 4
DK = 128
DV = 256
CHUNK = 128
GROUPS = 8
GD = 128
CONV_K = 5
HALO = 8
GATE_PAD = 128

VMEM_LIMIT = 56 * 1024 * 1024


def _sigmoid(x):
    return 1.0 / (1.0 + jnp.exp(-x))


def _log_sigmoid(x):
    return jnp.minimum(x, 0.0) - jnp.log(1.0 + jnp.exp(-jnp.abs(x)))


def _gelu_tanh(x):
    c = 0.7978845608028654
    return 0.5 * x * (1.0 + jnp.tanh(c * (x + 0.044715 * (x * x * x))))


def _rms(x, g):
    return x * lax.rsqrt(jnp.mean(x * x, axis=-1, keepdims=True) + EPS) * g


def _norm_mod(x, g, shift, scale):
    return _rms(x, g) * (1.0 + scale) + shift


def _dot(a, b):
    return jnp.dot(a, b, preferred_element_type=F32)


def _ada_kernel(c_ref, w_ref, b_ref, o_ref):
    c = c_ref[...]
    s = (c * _sigmoid(c)).astype(BF16)
    o_ref[...] = _dot(s, w_ref[...].astype(BF16)) + b_ref[...]


def _adaln(cond, w, b, tn=1024):
    rows, d = cond.shape
    n = w.shape[1]
    return pl.pallas_call(
        _ada_kernel,
        grid=(n // tn,),
        in_specs=[pl.BlockSpec((rows, d), lambda j: (0, 0)),
                  pl.BlockSpec((d, tn), lambda j: (0, j)),
                  pl.BlockSpec((1, tn), lambda j: (0, j))],
        out_specs=pl.BlockSpec((rows, tn), lambda j: (0, j)),
        out_shape=jax.ShapeDtypeStruct((rows, n), F32),
        compiler_params=pltpu.CompilerParams(dimension_semantics=("parallel",),
                                             vmem_limit_bytes=VMEM_LIMIT),
        name="adaln",
    )(cond, w, b.reshape(1, n))


def _ffn_kernel(*refs, mod_base, final):
    if final:
        x_ref, mod_ref, g_ref, wg_ref, wu_ref, wo_ref, fn_ref, o_ref, xn_ref, acc_ref = refs
    else:
        x_ref, mod_ref, g_ref, wg_ref, wu_ref, wo_ref, o_ref, xn_ref, acc_ref = refs
    f = pl.program_id(2)

    @pl.when(f == 0)
    def _():
        shift = mod_ref[0, mod_base:mod_base + 1, :]
        scale = mod_ref[0, mod_base + 1:mod_base + 2, :]
        xn_ref[...] = _norm_mod(x_ref[0], g_ref[...], shift, scale).astype(BF16)
        acc_ref[...] = jnp.zeros_like(acc_ref)

    xn = xn_ref[...]
    gate = _dot(xn, wg_ref[...])
    up = _dot(xn, wu_ref[...])
    h = (gate * _sigmoid(gate) * up).astype(BF16)
    acc_ref[...] += _dot(h, wo_ref[...])

    @pl.when(f == pl.num_programs(2) - 1)
    def _():
        y = x_ref[0] + 0.5 * mod_ref[0, mod_base + 2:mod_base + 3, :] * acc_ref[...]
        if final:
            y = _rms(y, fn_ref[...])
        o_ref[0] = y


def _ffn(x, mods, mod_row, mod_base, g, w_in, w_out, final_g=None, tm=512, tf=512):
    bsz, s, d = x.shape
    ff = w_out.shape[0]
    tm = min(tm, s)
    nf = ff // tf
    final = final_g is not None
    in_specs = [pl.BlockSpec((1, tm, d), lambda b, i, f: (b, i, 0)),
                pl.BlockSpec((1, N_MOD, d), lambda b, i, f: (mod_row(b), 0, 0)),
                pl.BlockSpec((1, d), lambda b, i, f: (0, 0)),
                pl.BlockSpec((d, tf), lambda b, i, f: (0, f)),
                pl.BlockSpec((d, tf), lambda b, i, f: (0, nf + f)),
                pl.BlockSpec((tf, d), lambda b, i, f: (f, 0))]
    args = [x, mods, g.reshape(1, d), w_in, w_in, w_out]
    if final:
        in_specs.append(pl.BlockSpec((1, d), lambda b, i, f: (0, 0)))
        args.append(final_g.reshape(1, d))
    return pl.pallas_call(
        functools.partial(_ffn_kernel, mod_base=mod_base, final=final),
        grid=(bsz, s // tm, nf),
        in_specs=in_specs,
        out_specs=pl.BlockSpec((1, tm, d), lambda b, i, f: (b, i, 0)),
        out_shape=jax.ShapeDtypeStruct((bsz, s, d), F32),
        scratch_shapes=[pltpu.VMEM((tm, d), BF16), pltpu.VMEM((tm, d), F32)],
        compiler_params=pltpu.CompilerParams(
            dimension_semantics=("parallel", "parallel", "arbitrary"),
            vmem_limit_bytes=VMEM_LIMIT),
        name="ffn_final" if final else "ffn",
    )(*args)


def _inproj_kernel(*refs, ctx_mode, tm):
    if ctx_mode:
        (x_ref, xp_ref, xq_ref, mod_ref, g_ref, wqk_ref, wv_ref, wg_ref, cw_ref, cb_ref,
         k_ref, v_ref, zg_ref, zs_ref) = refs
    else:
        (x_ref, xp_ref, xq_ref, mod_ref, g_ref, wqk_ref, wv_ref, wg_ref, cw_ref, cb_ref,
         wo_ref, wm_ref, gn_ref,
         q_ref, k_ref, v_ref, zg_ref, o_ref, u_ref, vn_ref, zs_ref) = refs
    i = pl.program_id(1)
    g = g_ref[...]
    shift = mod_ref[0, 3:4, :]
    scale = mod_ref[0, 4:5, :]

    xn = _norm_mod(x_ref[0], g, shift, scale).astype(BF16)
    wqk = wqk_ref[...]

    zp = _dot(_norm_mod(xp_ref[0], g, shift, scale).astype(BF16), wqk)
    zq = _dot(_norm_mod(xq_ref[0], g, shift, scale).astype(BF16), wqk)
    zs_ref[0:HALO, :] = jnp.where(i == 0, 0.0, zp)
    zs_ref[HALO + tm:2 * HALO + tm, :] = jnp.where(i == pl.num_programs(1) - 1, 0.0, zq)
    zs_ref[HALO:HALO + tm, :] = _dot(xn, wqk)
    conv = cb_ref[...]
    for j in range(CONV_K):
        off = HALO - CONV_K // 2 + j
        conv = conv + zs_ref[off:off + tm, :] * cw_ref[j:j + 1, :]
    qk = conv * _sigmoid(conv)
    if ctx_mode:
        k_ref[0] = qk.astype(BF16)
    else:
        half = qk.shape[1] // 2
        q_ref[0] = (qk[:, :half] * (DK ** -0.5)).astype(BF16)
        k_ref[0] = qk[:, half:].astype(BF16)

    v_ref[0] = _dot(xn, wv_ref[...]).astype(BF16)
    zg_ref[0] = _dot(xn, wg_ref[...])
    if not ctx_mode:
        o_ref[0] = _sigmoid(_dot(xn, wo_ref[...])).astype(BF16)
        gm = _gelu_tanh(_dot(xn, wm_ref[...]))
        gw = gm.shape[1] // 2
        u_ref[0] = gm[:, :gw].astype(BF16)
        vn_ref[0] = _rms(gm[:, gw:], gn_ref[...]).astype(BF16)


def _const_spec(shape):
    nd = len(shape)
    return pl.BlockSpec(shape, lambda b, i: (0,) * nd, pipeline_mode=pl.Buffered(1))


def _inproj(x, mods, mod_row, g, wqk, wv, wg, cw, cb, wo=None, wm=None, gn=None, tm=512):
    bsz, s, d = x.shape
    ctx_mode = wo is None
    tm = min(tm, s)
    nh = s // HALO
    r = tm // HALO
    wq = wqk.shape[1]
    in_specs = [pl.BlockSpec((1, tm, d), lambda b, i: (b, i, 0)),
                pl.BlockSpec((1, HALO, d), lambda b, i: (b, jnp.maximum(i * r - 1, 0), 0)),
                pl.BlockSpec((1, HALO, d), lambda b, i: (b, jnp.minimum((i + 1) * r, nh - 1), 0)),
                pl.BlockSpec((1, N_MOD, d), lambda b, i: (mod_row(b), 0, 0)),
                _const_spec((1, d)), _const_spec(wqk.shape), _const_spec(wv.shape),
                _const_spec(wg.shape), _const_spec(cw.shape), _const_spec(cb.shape)]
    args = [x, x, x, mods, g.reshape(1, d), wqk, wv, wg, cw, cb]

    def tok(width, dtype):
        return (pl.BlockSpec((1, tm, width), lambda b, i: (b, i, 0)),
                jax.ShapeDtypeStruct((bsz, s, width), dtype))

    if ctx_mode:
        outs = [tok(wq, BF16), tok(wv.shape[1], BF16), tok(GATE_PAD, F32)]
    else:
        in_specs += [_const_spec(wo.shape), _const_spec(wm.shape), _const_spec(gn.shape)]
        args += [wo, wm, gn]
        outs = [tok(wq // 2, BF16), tok(wq // 2, BF16), tok(wv.shape[1], BF16), tok(GATE_PAD, F32),
                tok(wo.shape[1], BF16), tok(wm.shape[1] // 2, BF16), tok(wm.shape[1] // 2, BF16)]
    return pl.pallas_call(
        functools.partial(_inproj_kernel, ctx_mode=ctx_mode, tm=tm),
        grid=(bsz, s // tm),
        in_specs=in_specs,
        out_specs=[o[0] for o in outs],
        out_shape=[o[1] for o in outs],
        scratch_shapes=[pltpu.VMEM((tm + 2 * HALO, wq), F32)],
        compiler_params=pltpu.CompilerParams(dimension_semantics=("parallel", "parallel"),
                                             vmem_limit_bytes=VMEM_LIMIT),
        name="inproj_ctx" if ctx_mode else "inproj",
    )(*args)


def _mlstm_kernel(q_ref, k_ref, v_ref, o_ref, gr_ref, kc_ref, vc_ref, grc_ref,
                  br_ref, ng_ref, out_ref, h_ref, c_ref, n_ref, m_ref):
    s = q_ref.shape[1]
    nc = s // CHUNK
    ncc = kc_ref.shape[1] // CHUNK
    row = lax.broadcasted_iota(jnp.int32, (CHUNK, CHUNK), 0)
    col = lax.broadcasted_iota(jnp.int32, (CHUNK, CHUNK), 1)
    eye = row == col
    bias_r = br_ref[0]

    def to_col(r):
        return jnp.sum(jnp.where(eye, r, 0.0), axis=-1, keepdims=True)

    def to_row(c):
        return jnp.sum(jnp.where(eye, c, 0.0), axis=0, keepdims=True)

    def gates(gr, d):
        gr = gr + bias_r
        logi_r = gr[2 * d:2 * d + 1, :]
        logf_r = _log_sigmoid(gr[2 * d + 1:2 * d + 2, :])
        inc = (col <= row) if d == 0 else (col >= row)
        b_c = jnp.sum(jnp.where(inc, logf_r, 0.0), axis=-1, keepdims=True)
        b_r = to_row(b_c)
        logi_c = to_col(logi_r)
        tot = jnp.sum(logf_r, axis=-1, keepdims=True)
        return logi_c, logi_r, b_c, b_r, tot, inc

    def update(k, v, logi_c, b_c, tot):
        m = m_ref[...]
        g_c = tot - b_c + logi_c
        m_new = jnp.maximum(tot + m, jnp.max(g_c, axis=0, keepdims=True))
        decay = jnp.exp(tot + m - m_new)
        kw = k.astype(F32) * jnp.exp(g_c - m_new)
        c_ref[...] = decay * c_ref[...] + lax.dot_general(
            kw.astype(BF16), v, (((0,), (0,)), ((), ())), preferred_element_type=F32)
        n_ref[...] = decay * n_ref[...] + jnp.sum(kw, axis=0, keepdims=True)
        m_ref[...] = m_new

    def output(q, k, v, logi_r, b_c, b_r, inc):
        m = m_ref[...]
        d_mat = jnp.where(inc, b_c - b_r + logi_r, -jnp.inf)
        m_inter = b_c + m
        m_tok = jnp.maximum(m_inter, jnp.max(d_mat, axis=-1, keepdims=True))
        qk = lax.dot_general(q, k, (((1,), (1,)), ((), ())), preferred_element_type=F32)
        sc = qk * jnp.exp(d_mat - m_tok)
        a = jnp.exp(m_inter - m_tok)
        num = a * _dot(q, c_ref[...].astype(BF16)) + _dot(sc.astype(BF16), v)
        nq = (a * jnp.sum(q.astype(F32) * n_ref[...], axis=-1, keepdims=True)
              + jnp.sum(sc, axis=-1, keepdims=True))
        return num / jnp.maximum(jnp.abs(nq), jnp.exp(-m_tok))

    for d in range(2):
        c_ref[...] = jnp.zeros_like(c_ref)
        n_ref[...] = jnp.zeros_like(n_ref)
        m_ref[...] = jnp.zeros_like(m_ref)
        for t in range(ncc):
            ci = t if d == 0 else ncc - 1 - t
            rows = pl.ds(ci * CHUNK, CHUNK)
            logi_c, _, b_c, _, tot, _ = gates(grc_ref[0, 0, ci], d)
            update(kc_ref[0, rows, :], vc_ref[0, rows, :], logi_c, b_c, tot)

        def body(t, carry, d=d):
            ci = t if d == 0 else nc - 1 - t
            rows = pl.ds(pl.multiple_of(ci * CHUNK, CHUNK), CHUNK)
            logi_c, logi_r, b_c, b_r, tot, inc = gates(gr_ref[0, 0, ci], d)
            q = q_ref[0, rows, :]
            k = k_ref[0, rows, :]
            v = v_ref[0, rows, :]
            h = output(q, k, v, logi_r, b_c, b_r, inc)
            if d == 0:
                h_ref[rows, :] = h
            else:
                h_ref[rows, :] += h
            update(k, v, logi_c, b_c, tot)
            return carry

        lax.fori_loop(0, nc, body, 0)

    def finish(t, carry):
        rows = pl.ds(pl.multiple_of(t * CHUNK, CHUNK), CHUNK)
        out_ref[0, rows, :] = (_rms(h_ref[rows, :], ng_ref[...]) * o_ref[0, rows, :].astype(F32)).astype(BF16)
        return carry

    lax.fori_loop(0, nc, finish, 0)


def _mlstm(q, k, v, o, gr, kc, vc, grc, bias_r, norm_g):
    bsz, s, _ = q.shape
    sc = kc.shape[1]
    nc, ncc = s // CHUNK, sc // CHUNK
    return pl.pallas_call(
        _mlstm_kernel,
        grid=(bsz, HEADS),
        in_specs=[pl.BlockSpec((1, s, DK), lambda b, h: (b, 0, h)),
                  pl.BlockSpec((1, s, DK), lambda b, h: (b, 0, h)),
                  pl.BlockSpec((1, s, DV), lambda b, h: (b, 0, h)),
                  pl.BlockSpec((1, s, DV), lambda b, h: (b, 0, h)),
                  pl.BlockSpec((1, 1, nc, 4, CHUNK), lambda b, h: (b, h, 0, 0, 0)),
                  pl.BlockSpec((1, sc, DK), lambda b, h: (b, 0, h)),
                  pl.BlockSpec((1, sc, DV), lambda b, h: (b, 0, h)),
                  pl.BlockSpec((1, 1, ncc, 4, CHUNK), lambda b, h: (b, h, 0, 0, 0)),
                  pl.BlockSpec((1, 4, 1), lambda b, h: (h, 0, 0)),
                  pl.BlockSpec((1, DV), lambda b, h: (0, h))],
        out_specs=pl.BlockSpec((1, s, DV), lambda b, h: (b, 0, h)),
        out_shape=jax.ShapeDtypeStruct((bsz, s, HEADS * DV), BF16),
        scratch_shapes=[pltpu.VMEM((s, DV), F32), pltpu.VMEM((DK, DV), F32),
                        pltpu.VMEM((1, DK), F32), pltpu.VMEM((1, 1), F32)],
        compiler_params=pltpu.CompilerParams(dimension_semantics=("parallel", "parallel"),
                                             vmem_limit_bytes=VMEM_LIMIT),
        name="mlstm",
    )(q, k, v, o, gr, kc, vc, grc, bias_r, norm_g)


def _outproj_kernel(x_ref, mod_ref, hx_ref, u_ref, vn_ref, gw_ref, gb_ref, wt_ref, wb_ref, o_ref, gx_ref):
    tm = x_ref.shape[1]
    for ci in range(tm // CHUNK):
        rows = slice(ci * CHUNK, (ci + 1) * CHUNK)
        for g in range(GROUPS):
            cols = slice(g * GD, (g + 1) * GD)
            sp = _dot(gw_ref[g], vn_ref[0, rows, cols]) + gb_ref[:, cols]
            gx_ref[rows, cols] = (u_ref[0, rows, cols].astype(F32) * sp).astype(BF16)
    y = _dot(hx_ref[0], wt_ref[...]) + _dot(gx_ref[...], wb_ref[...])
    o_ref[0] = x_ref[0] + mod_ref[0, 5:6, :] * y


def _outproj(x, mods, hx, u, vn, gw, gb, wt, wb, tm=512):
    bsz, s, d = x.shape
    w = hx.shape[2]
    return pl.pallas_call(
        _outproj_kernel,
        grid=(bsz, s // tm),
        in_specs=[pl.BlockSpec((1, tm, d), lambda b, i: (b, i, 0)),
                  pl.BlockSpec((1, N_MOD, d), lambda b, i: (b, 0, 0)),
                  pl.BlockSpec((1, tm, w), lambda b, i: (b, i, 0)),
                  pl.BlockSpec((1, tm, w), lambda b, i: (b, i, 0)),
                  pl.BlockSpec((1, tm, w), lambda b, i: (b, i, 0)),
                  _const_spec(gw.shape), _const_spec(gb.shape),
                  _const_spec(wt.shape), _const_spec(wb.shape)],
        out_specs=pl.BlockSpec((1, tm, d), lambda b, i: (b, i, 0)),
        out_shape=jax.ShapeDtypeStruct((bsz, s, d), F32),
        scratch_shapes=[pltpu.VMEM((tm, w), BF16)],
        compiler_params=pltpu.CompilerParams(dimension_semantics=("parallel", "parallel"),
                                             vmem_limit_bytes=VMEM_LIMIT),
        name="outproj",
    )(x, mods, hx, u, vn, gw, gb, wt, wb)


def _gate_rows(zg):
    bsz, t, _ = zg.shape
    g = zg[:, :, :4 * HEADS].reshape(bsz, t // CHUNK, CHUNK, 4, HEADS)
    return g.transpose(0, 4, 1, 3, 2)


def kernel(x, c, ctx, c_ctx, w_ada, b_ada, norm_ffn1, w_ffn1_in, w_ffn1_out, norm_mix, w_in, conv_w, conv_b, b_igate, b_fgate, mlstm_norm, gmlp_norm, gmlp_w, gmlp_b, w_out, norm_ffn2, w_ffn2_in, w_ffn2_out, final_norm):
    bsz, s, d = x.shape
    depth = w_ada.shape[0]
    assert depth == 1, "context-stream update for non-final layers is not implemented"
    l = 0
    qk_w = 2 * HEADS * DK
    mw = HEADS * DV
    c_qk, c_v, c_o = qk_w, qk_w + mw, qk_w + 2 * mw
    c_gate = c_o + 4 * HEADS

    lat_row = lambda b: b
    ctx_row = lambda b: bsz

    rows = -(-(bsz + 1) // 8) * 8
    cond = jnp.zeros((rows, d), F32).at[:bsz].set(c).at[bsz].set(c_ctx)
    mods = _adaln(cond, w_ada[l], b_ada[l]).reshape(rows, N_MOD, d)

    w1i, w1o = w_ffn1_in[l].astype(BF16), w_ffn1_out[l].astype(BF16)
    w2i, w2o = w_ffn2_in[l].astype(BF16), w_ffn2_out[l].astype(BF16)
    wi = w_in[l].astype(BF16)
    wqk, wv, wo = wi[:, :c_qk], wi[:, c_qk:c_v], wi[:, c_v:c_o]
    wg = jnp.pad(wi[:, c_o:c_gate], ((0, 0), (0, GATE_PAD - 4 * HEADS)))
    wm = wi[:, c_gate:]
    cw = jnp.pad(conv_w[l], ((0, 8 - CONV_K), (0, 0)))
    cb = conv_b[l].reshape(1, qk_w)
    wout = w_out[l].astype(BF16)

    x = _ffn(x, mods, lat_row, 0, norm_ffn1[l], w1i, w1o)
    ctx = _ffn(ctx, mods, ctx_row, 0, norm_ffn1[l], w1i, w1o)

    q, k, v, zg, o, u, vn = _inproj(x, mods, lat_row, norm_mix[l], wqk, wv, wg, cw, cb,
                                    wo, wm, gmlp_norm[l].reshape(1, -1))
    kc, vc, zgc = _inproj(ctx, mods, ctx_row, norm_mix[l], wqk[:, qk_w // 2:], wv, wg,
                          cw[:, qk_w // 2:], cb[:, qk_w // 2:])

    bias = jnp.stack([b_igate[l][0], b_fgate[l][0], b_igate[l][1], b_fgate[l][1]], axis=-1)
    hx = _mlstm(q, k, v, o, _gate_rows(zg), kc, vc, _gate_rows(zgc), bias.reshape(HEADS, 4, 1),
                mlstm_norm[l].reshape(1, mw))

    gb = jnp.repeat(gmlp_b[l].T, GD, axis=1)
    x = _outproj(x, mods, hx, u, vn, gmlp_w[l].astype(BF16), gb, wout[:mw], wout[mw:])

    return _ffn(x, mods, lat_row, 6, norm_ffn2[l], w2i, w2o, final_g=final_norm)
```

```python
import functools

import jax
import jax.numpy as jnp
from jax import lax
from jax.experimental import pallas as pl
from jax.experimental.pallas import tpu as pltpu

F32 = jnp.float32
BF16 = jnp.bfloat16

EPS = 1e-6
N_MOD = 9
HEADS = 4
DK = 128
DV = 256
CHUNK = 128
GROUPS = 8
GD = 128
CONV_K = 5
HALO = 8
GATE_PAD = 128
CTX_ROWS = 8
NORM_COLS = 128

VMEM_LIMIT = 56 * 1024 * 1024


def _sigmoid(x):
    return 1.0 / (1.0 + jnp.exp(-x))


def _log_sigmoid(x):
    return jnp.minimum(x, 0.0) - jnp.log(1.0 + jnp.exp(-jnp.abs(x)))


def _gelu_tanh(x):
    c = 0.7978845608028654
    return 0.5 * x * (1.0 + jnp.tanh(c * (x + 0.044715 * (x * x * x))))


def _rms(x, g):
    return x * lax.rsqrt(jnp.mean(x * x, axis=-1, keepdims=True) + EPS) * g


def _norm_mod(x, g, shift, scale):
    return _rms(x, g) * (1.0 + scale) + shift


def _dot(a, b):
    return jnp.dot(a, b, preferred_element_type=F32)


def _ada_kernel(c_ref, w_ref, b_ref, o_ref):
    c = c_ref[...]
    s = (c * _sigmoid(c)).astype(BF16)
    o_ref[...] = _dot(s, w_ref[...].astype(BF16)) + b_ref[...]


def _adaln(cond, w, b, tn=1024):
    rows, d = cond.shape
    n = w.shape[1]
    return pl.pallas_call(
        _ada_kernel,
        grid=(n // tn,),
        in_specs=[pl.BlockSpec((rows, d), lambda j: (0, 0)),
                  pl.BlockSpec((d, tn), lambda j: (0, j)),
                  pl.BlockSpec((1, tn), lambda j: (0, j))],
        out_specs=pl.BlockSpec((rows, tn), lambda j: (0, j)),
        out_shape=jax.ShapeDtypeStruct((rows, n), F32),
        compiler_params=pltpu.CompilerParams(dimension_semantics=("parallel",),
                                             vmem_limit_bytes=VMEM_LIMIT),
        name="adaln",
    )(cond, w, b.reshape(1, n))


def _ffn_kernel(*refs, mod_base, final):
    if final:
        x_ref, mod_ref, g_ref, wg_ref, wu_ref, wo_ref, fn_ref, o_ref, xn_ref, acc_ref = refs
    else:
        x_ref, mod_ref, g_ref, wg_ref, wu_ref, wo_ref, o_ref, xn_ref, acc_ref = refs
    f = pl.program_id(2)

    @pl.when(f == 0)
    def _():
        shift = mod_ref[0, mod_base:mod_base + 1, :]
        scale = mod_ref[0, mod_base + 1:mod_base + 2, :]
        xn_ref[...] = _norm_mod(x_ref[0], g_ref[...], shift, scale).astype(BF16)
        acc_ref[...] = jnp.zeros_like(acc_ref)

    xn = xn_ref[...]
    gate = _dot(xn, wg_ref[...])
    up = _dot(xn, wu_ref[...])
    h = (gate * _sigmoid(gate) * up).astype(BF16)
    acc_ref[...] += _dot(h, wo_ref[...])

    @pl.when(f == pl.num_programs(2) - 1)
    def _():
        y = x_ref[0] + 0.5 * mod_ref[0, mod_base + 2:mod_base + 3, :] * acc_ref[...]
        if final:
            y = _rms(y, fn_ref[...])
        o_ref[0] = y


def _ffn(x, mods, mod_row, mod_base, g, w_in, w_out, final_g=None, tm=512, tf=512):
    bsz, s, d = x.shape
    ff = w_out.shape[0]
    tm = min(tm, s)
    nf = ff // tf
    final = final_g is not None
    in_specs = [pl.BlockSpec((1, tm, d), lambda b, i, f: (b, i, 0)),
                pl.BlockSpec((1, N_MOD, d), lambda b, i, f: (mod_row(b), 0, 0)),
                pl.BlockSpec((1, d), lambda b, i, f: (0, 0)),
                pl.BlockSpec((d, tf), lambda b, i, f: (0, f)),
                pl.BlockSpec((d, tf), lambda b, i, f: (0, nf + f)),
                pl.BlockSpec((tf, d), lambda b, i, f: (f, 0))]
    args = [x, mods, g.reshape(1, d), w_in, w_in, w_out]
    if final:
        in_specs.append(pl.BlockSpec((1, d), lambda b, i, f: (0, 0)))
        args.append(final_g.reshape(1, d))
    return pl.pallas_call(
        functools.partial(_ffn_kernel, mod_base=mod_base, final=final),
        grid=(bsz, s // tm, nf),
        in_specs=in_specs,
        out_specs=pl.BlockSpec((1, tm, d), lambda b, i, f: (b, i, 0)),
        out_shape=jax.ShapeDtypeStruct((bsz, s, d), F32),
        scratch_shapes=[pltpu.VMEM((tm, d), BF16), pltpu.VMEM((tm, d), F32)],
        compiler_params=pltpu.CompilerParams(
            dimension_semantics=("parallel", "parallel", "arbitrary"),
            vmem_limit_bytes=VMEM_LIMIT),
        name="ffn_final" if final else "ffn",
    )(*args)


def _inproj_kernel(*refs, ctx_mode, tm):
    if ctx_mode:
        (x_ref, xp_ref, xq_ref, mod_ref, g_ref, wqk_ref, wv_ref, wg_ref, cw_ref, cb_ref,
         k_ref, v_ref, zg_ref, zs_ref) = refs
    else:
        (x_ref, xp_ref, xq_ref, mod_ref, g_ref, wqk_ref, wv_ref, wg_ref, cw_ref, cb_ref,
         wo_ref, wm_ref, gn_ref,
         q_ref, k_ref, v_ref, zg_ref, o_ref, u_ref, vn_ref, zs_ref) = refs
    i = pl.program_id(1)
    g = g_ref[...]
    shift = mod_ref[0, 3:4, :]
    scale = mod_ref[0, 4:5, :]

    xn = _norm_mod(x_ref[0], g, shift, scale).astype(BF16)
    wqk = wqk_ref[...]

    zp = _dot(_norm_mod(xp_ref[0], g, shift, scale).astype(BF16), wqk)
    zq = _dot(_norm_mod(xq_ref[0], g, shift, scale).astype(BF16), wqk)
    zs_ref[0:HALO, :] = jnp.where(i == 0, 0.0, zp)
    zs_ref[HALO + tm:2 * HALO + tm, :] = jnp.where(i == pl.num_programs(1) - 1, 0.0, zq)
    zs_ref[HALO:HALO + tm, :] = _dot(xn, wqk)
    conv = cb_ref[...]
    for j in range(CONV_K):
        off = HALO - CONV_K // 2 + j
        conv = conv + zs_ref[off:off + tm, :] * cw_ref[j:j + 1, :]
    qk = conv * _sigmoid(conv)
    if ctx_mode:
        k = qk
    else:
        half = qk.shape[1] // 2
        q_ref[0] = (qk[:, :half] * (DK ** -0.5)).astype(BF16)
        k = qk[:, half:]
    for h in range(HEADS):
        for cc in range(tm // CHUNK):
            k_ref[0, h, cc] = k[cc * CHUNK:(cc + 1) * CHUNK, h * DK:(h + 1) * DK].T.astype(BF16)

    v_ref[0] = _dot(xn, wv_ref[...]).astype(BF16)
    zg_ref[0] = _dot(xn, wg_ref[...])
    if not ctx_mode:
        o_ref[0] = _sigmoid(_dot(xn, wo_ref[...])).astype(BF16)
        gm = _gelu_tanh(_dot(xn, wm_ref[...]))
        gw = gm.shape[1] // 2
        u_ref[0] = gm[:, :gw].astype(BF16)
        vn_ref[0] = _rms(gm[:, gw:], gn_ref[...]).astype(BF16)


def _const_spec(shape):
    nd = len(shape)
    return pl.BlockSpec(shape, lambda b, i: (0,) * nd, pipeline_mode=pl.Buffered(1))


def _inproj(x, mods, mod_row, g, wqk, wv, wg, cw, cb, wo=None, wm=None, gn=None, tm=512):
    bsz, s, d = x.shape
    ctx_mode = wo is None
    tm = min(tm, s)
    nh = s // HALO
    r = tm // HALO
    wq = wqk.shape[1]
    in_specs = [pl.BlockSpec((1, tm, d), lambda b, i: (b, i, 0)),
                pl.BlockSpec((1, HALO, d), lambda b, i: (b, jnp.maximum(i * r - 1, 0), 0)),
                pl.BlockSpec((1, HALO, d), lambda b, i: (b, jnp.minimum((i + 1) * r, nh - 1), 0)),
                pl.BlockSpec((1, N_MOD, d), lambda b, i: (mod_row(b), 0, 0)),
                _const_spec((1, d)), _const_spec(wqk.shape), _const_spec(wv.shape),
                _const_spec(wg.shape), _const_spec(cw.shape), _const_spec(cb.shape)]
    args = [x, x, x, mods, g.reshape(1, d), wqk, wv, wg, cw, cb]

    def tok(width, dtype):
        return (pl.BlockSpec((1, tm, width), lambda b, i: (b, i, 0)),
                jax.ShapeDtypeStruct((bsz, s, width), dtype))

    kt = (pl.BlockSpec((1, HEADS, tm // CHUNK, DK, CHUNK), lambda b, i: (b, 0, i, 0, 0)),
          jax.ShapeDtypeStruct((bsz, HEADS, s // CHUNK, DK, CHUNK), BF16))
    if ctx_mode:
        outs = [kt, tok(wv.shape[1], BF16), tok(GATE_PAD, F32)]
    else:
        in_specs += [_const_spec(wo.shape), _const_spec(wm.shape), _const_spec(gn.shape)]
        args += [wo, wm, gn]
        outs = [tok(wq // 2, BF16), kt, tok(wv.shape[1], BF16), tok(GATE_PAD, F32),
                tok(wo.shape[1], BF16), tok(wm.shape[1] // 2, BF16), tok(wm.shape[1] // 2, BF16)]
    return pl.pallas_call(
        functools.partial(_inproj_kernel, ctx_mode=ctx_mode, tm=tm),
        grid=(bsz, s // tm),
        in_specs=in_specs,
        out_specs=[o[0] for o in outs],
        out_shape=[o[1] for o in outs],
        scratch_shapes=[pltpu.VMEM((tm + 2 * HALO, wq), F32)],
        compiler_params=pltpu.CompilerParams(dimension_semantics=("parallel", "parallel"),
                                             vmem_limit_bytes=VMEM_LIMIT),
        name="inproj_ctx" if ctx_mode else "inproj",
    )(*args)


def _lane_scan(x, op, ident, reverse):
    lane = lax.broadcasted_iota(jnp.int32, x.shape, 1)
    n = x.shape[1]
    k = 1
    while k < n:
        if reverse:
            x = op(x, jnp.where(lane < n - k, pltpu.roll(x, n - k, axis=1), ident))
        else:
            x = op(x, jnp.where(lane >= k, pltpu.roll(x, k, axis=1), ident))
        k *= 2
    return x


def _mlstm_kernel(q_ref, kt_ref, v_ref, o_ref, g_ref, ktc_ref, vc_ref, br_ref, ng_ref, out_ref,
                  e_ref, col_ref, tot_ref, emax_ref, cs_ref, ms_ref, c_ref, m_ref):
    s = q_ref.shape[1]
    nc = s // CHUNK
    ncc = vc_ref.shape[1] // CHUNK
    row = lax.broadcasted_iota(jnp.int32, (CHUNK, CHUNK), 0)
    col = lax.broadcasted_iota(jnp.int32, (CHUNK, CHUNK), 1)
    eye = row == col
    ones = jnp.ones((CHUNK, NORM_COLS), BF16)
    sel_r = lax.broadcasted_iota(jnp.int32, (3 * CHUNK, 2 * NORM_COLS), 0)
    sel_c = lax.broadcasted_iota(jnp.int32, (3 * CHUNK, 2 * NORM_COLS), 1)
    sel = jnp.where((sel_r < CHUNK) == (sel_c < NORM_COLS), 1.0, 0.0).astype(BF16)

    for d in range(2):
        logi = g_ref[0, 0, 2 * d] + br_ref[0, 2 * d]
        logf = _log_sigmoid(g_ref[0, 0, 2 * d + 1] + br_ref[0, 2 * d + 1])
        b = _lane_scan(logf, jnp.add, 0.0, d == 1)
        e = logi - b
        e_ref[d] = e
        b_hi = b.astype(BF16).astype(F32)
        col_ref[d, 0] = _lane_scan(e, jnp.maximum, -jnp.inf, d == 1).astype(BF16).astype(F32)
        col_ref[d, 1] = b_hi
        col_ref[d, 2] = (b - b_hi).astype(BF16).astype(F32)
        tot_ref[d] = jnp.sum(logf, axis=-1, keepdims=True)
        emax_ref[d] = jnp.max(e, axis=-1, keepdims=True)

    def scan_step(d, r, kt, v):
        e_r = e_ref[d, pl.ds(r, 1), :]
        emax = emax_ref[d, pl.ds(r, 1), :]
        kw = (kt.astype(F32) * jnp.exp(e_r - emax)).astype(BF16)
        upd = _dot(kw, jnp.concatenate([v, ones], axis=1))
        m = m_ref[d]
        c = c_ref[d]
        mx = jnp.maximum(m, emax)
        c_ref[d] = jnp.exp(m - mx) * c + jnp.exp(emax - mx) * upd
        m_ref[d] = tot_ref[d, pl.ds(r, 1), :] + mx
        return c.astype(BF16), m

    c_ref[...] = jnp.zeros_like(c_ref)
    m_ref[...] = jnp.zeros_like(m_ref)
    for t in range(ncc):
        for d in range(2):
            ci = t if d == 0 else ncc - 1 - t
            scan_step(d, ci, ktc_ref[0, 0, ci], vc_ref[0, ci * CHUNK:(ci + 1) * CHUNK, :])

    def scan_body(t, carry):
        for d in range(2):
            ci = t if d == 0 else nc - 1 - t
            rows = pl.ds(pl.multiple_of(ci * CHUNK, CHUNK), CHUNK)
            c_in, m_in = scan_step(d, CTX_ROWS + ci, kt_ref[0, 0, ci], v_ref[0, rows, :])
            cs_ref[d, ci] = c_in
            ms_ref[d, ci] = m_in
        return carry

    lax.fori_loop(0, nc, scan_body, 0, unroll=4)

    def out_body(ci, carry):
        rows = pl.ds(pl.multiple_of(ci * CHUNK, CHUNK), CHUNK)
        q = q_ref[0, rows, :]
        qf = q.astype(F32)
        v_aug = jnp.concatenate([v_ref[0, rows, :], ones], axis=1)
        qk = _dot(q, kt_ref[0, 0, ci])
        h = None
        for d in range(2):
            r = pl.ds(CTX_ROWS + ci, 1)
            inc = (col <= row) if d == 0 else (col >= row)
            e_m = jnp.where(inc, e_ref[d, r, :], -jnp.inf)
            diag = jnp.concatenate([jnp.where(eye, col_ref[d, i, r, :], 0.0) for i in range(3)], axis=1)
            rep = _dot(diag.astype(BF16), sel)
            m = ms_ref[d, ci]
            mm = jnp.maximum(m, rep[:, :NORM_COLS])
            b_q = rep[:, NORM_COLS:]
            p = (qk * jnp.exp(e_m - mm)).astype(BF16)
            qa = (qf * jnp.exp(m - mm)).astype(BF16)
            comb = _dot(qa, cs_ref[d, ci]) + _dot(p, v_aug)
            inv = 1.0 / jnp.maximum(jnp.abs(comb[:, DV:]), jnp.exp(-(b_q + mm)))
            hd = comb[:, :DV] * jnp.concatenate([inv] * (DV // NORM_COLS), axis=1)
            h = hd if h is None else h + hd
        out_ref[0, rows, :] = (_rms(h, ng_ref[...]) * o_ref[0, rows, :].astype(F32)).astype(BF16)
        return carry

    lax.fori_loop(0, nc, out_body, 0, unroll=2)


def _mlstm(q, kt, v, o, g, ktc, vc, bias, norm_g):
    bsz, s, _ = q.shape
    sc = vc.shape[1]
    nc, ncc = s // CHUNK, sc // CHUNK
    nr = g.shape[3]
    return pl.pallas_call(
        _mlstm_kernel,
        grid=(bsz, HEADS),
        in_specs=[pl.BlockSpec((1, s, DK), lambda b, h: (b, 0, h)),
                  pl.BlockSpec((1, 1, nc, DK, CHUNK), lambda b, h: (b, h, 0, 0, 0)),
                  pl.BlockSpec((1, s, DV), lambda b, h: (b, 0, h)),
                  pl.BlockSpec((1, s, DV), lambda b, h: (b, 0, h)),
                  pl.BlockSpec((1, 1, 4, nr, CHUNK), lambda b, h: (b, h, 0, 0, 0)),
                  pl.BlockSpec((1, 1, ncc, DK, CHUNK), lambda b, h: (b, h, 0, 0, 0)),
                  pl.BlockSpec((1, sc, DV), lambda b, h: (b, 0, h)),
                  pl.BlockSpec((1, 4, 1, 1), lambda b, h: (h, 0, 0, 0)),
                  pl.BlockSpec((1, DV), lambda b, h: (0, h))],
        out_specs=pl.BlockSpec((1, s, DV), lambda b, h: (b, 0, h)),
        out_shape=jax.ShapeDtypeStruct((bsz, s, HEADS * DV), BF16),
        scratch_shapes=[pltpu.VMEM((2, nr, CHUNK), F32), pltpu.VMEM((2, 3, nr, CHUNK), F32),
                        pltpu.VMEM((2, nr, 1), F32), pltpu.VMEM((2, nr, 1), F32),
                        pltpu.VMEM((2, nc, DK, DV + NORM_COLS), BF16), pltpu.VMEM((2, nc, 1, 1), F32),
                        pltpu.VMEM((2, DK, DV + NORM_COLS), F32), pltpu.VMEM((2, 1, 1), F32)],
        compiler_params=pltpu.CompilerParams(dimension_semantics=("parallel", "parallel"),
                                             vmem_limit_bytes=VMEM_LIMIT),
        name="mlstm",
    )(q, kt, v, o, g, ktc, vc, bias, norm_g)


def _outproj_kernel(x_ref, mod_ref, hx_ref, u_ref, vn_ref, gw_ref, gb_ref, wt_ref, wb_ref, o_ref, gx_ref):
    tm = x_ref.shape[1]
    for ci in range(tm // CHUNK):
        rows = slice(ci * CHUNK, (ci + 1) * CHUNK)
        for g in range(GROUPS):
            cols = slice(g * GD, (g + 1) * GD)
            sp = _dot(gw_ref[g], vn_ref[0, rows, cols]) + gb_ref[:, cols]
            gx_ref[rows, cols] = (u_ref[0, rows, cols].astype(F32) * sp).astype(BF16)
    y = _dot(hx_ref[0], wt_ref[...]) + _dot(gx_ref[...], wb_ref[...])
    o_ref[0] = x_ref[0] + mod_ref[0, 5:6, :] * y


def _outproj(x, mods, hx, u, vn, gw, gb, wt, wb, tm=512):
    bsz, s, d = x.shape
    w = hx.shape[2]
    return pl.pallas_call(
        _outproj_kernel,
        grid=(bsz, s // tm),
        in_specs=[pl.BlockSpec((1, tm, d), lambda b, i: (b, i, 0)),
                  pl.BlockSpec((1, N_MOD, d), lambda b, i: (b, 0, 0)),
                  pl.BlockSpec((1, tm, w), lambda b, i: (b, i, 0)),
                  pl.BlockSpec((1, tm, w), lambda b, i: (b, i, 0)),
                  pl.BlockSpec((1, tm, w), lambda b, i: (b, i, 0)),
                  _const_spec(gw.shape), _const_spec(gb.shape),
                  _const_spec(wt.shape), _const_spec(wb.shape)],
        out_specs=pl.BlockSpec((1, tm, d), lambda b, i: (b, i, 0)),
        out_shape=jax.ShapeDtypeStruct((bsz, s, d), F32),
        scratch_shapes=[pltpu.VMEM((tm, w), BF16)],
        compiler_params=pltpu.CompilerParams(dimension_semantics=("parallel", "parallel"),
                                             vmem_limit_bytes=VMEM_LIMIT),
        name="outproj",
    )(x, mods, hx, u, vn, gw, gb, wt, wb)


def _gate_rows(zgc, zg):
    def rows(z):
        bsz, t, _ = z.shape
        g = z[:, :, :4 * HEADS].reshape(bsz, t // CHUNK, CHUNK, 4, HEADS)
        return g.transpose(0, 4, 3, 1, 2)
    gc = rows(zgc)
    assert gc.shape[3] <= CTX_ROWS
    gc = jnp.pad(gc, ((0, 0), (0, 0), (0, 0), (0, CTX_ROWS - gc.shape[3]), (0, 0)))
    return jnp.concatenate([gc, rows(zg)], axis=3)


def kernel(x, c, ctx, c_ctx, w_ada, b_ada, norm_ffn1, w_ffn1_in, w_ffn1_out, norm_mix, w_in, conv_w, conv_b, b_igate, b_fgate, mlstm_norm, gmlp_norm, gmlp_w, gmlp_b, w_out, norm_ffn2, w_ffn2_in, w_ffn2_out, final_norm):
    bsz, s, d = x.shape
    depth = w_ada.shape[0]
    assert depth == 1, "context-stream update for non-final layers is not implemented"
    l = 0
    qk_w = 2 * HEADS * DK
    mw = HEADS * DV
    c_qk, c_v, c_o = qk_w, qk_w + mw, qk_w + 2 * mw
    c_gate = c_o + 4 * HEADS

    lat_row = lambda b: b
    ctx_row = lambda b: bsz

    rows = -(-(bsz + 1) // 8) * 8
    cond = jnp.zeros((rows, d), F32).at[:bsz].set(c).at[bsz].set(c_ctx)
    mods = _adaln(cond, w_ada[l], b_ada[l]).reshape(rows, N_MOD, d)

    w1i, w1o = w_ffn1_in[l].astype(BF16), w_ffn1_out[l].astype(BF16)
    w2i, w2o = w_ffn2_in[l].astype(BF16), w_ffn2_out[l].astype(BF16)
    wi = w_in[l].astype(BF16)
    wqk, wv, wo = wi[:, :c_qk], wi[:, c_qk:c_v], wi[:, c_v:c_o]
    wg = jnp.pad(wi[:, c_o:c_gate], ((0, 0), (0, GATE_PAD - 4 * HEADS)))
    wm = wi[:, c_gate:]
    cw = jnp.pad(conv_w[l], ((0, 8 - CONV_K), (0, 0)))
    cb = conv_b[l].reshape(1, qk_w)
    wout = w_out[l].astype(BF16)

    x = _ffn(x, mods, lat_row, 0, norm_ffn1[l], w1i, w1o)
    ctx = _ffn(ctx.reshape(1, -1, d), mods, ctx_row, 0, norm_ffn1[l], w1i, w1o).reshape(ctx.shape)

    q, kt, v, zg, o, u, vn = _inproj(x, mods, lat_row, norm_mix[l], wqk, wv, wg, cw, cb,
                                     wo, wm, gmlp_norm[l].reshape(1, -1))
    ktc, vc, zgc = _inproj(ctx, mods, ctx_row, norm_mix[l], wqk[:, qk_w // 2:], wv, wg,
                           cw[:, qk_w // 2:], cb[:, qk_w // 2:])

    bias = jnp.stack([b_igate[l][0], b_fgate[l][0], b_igate[l][1], b_fgate[l][1]], axis=-1)
    hx = _mlstm(q, kt, v, o, _gate_rows(zgc, zg), ktc, vc, bias.reshape(HEADS, 4, 1, 1),
                mlstm_norm[l].reshape(1, mw))

    gb = jnp.repeat(gmlp_b[l].T, GD, axis=1)
    x = _outproj(x, mods, hx, u, vn, gmlp_w[l].astype(BF16), gb, wout[:mw], wout[mw:])

    return _ffn(x, mods, lat_row, 6, norm_ffn2[l], w2i, w2o, final_g=final_norm)
```

```python
import functools

import jax
import jax.numpy as jnp
from jax import lax
from jax.experimental import pallas as pl
from jax.experimental.pallas import tpu as pltpu

F32 = jnp.float32
BF16 = jnp.bfloat16

EPS = 1e-6
N_MOD = 9
HEADS = 4
DK = 128
DV = 256
CHUNK = 128
GROUPS = 8
GD = 128
CONV_K = 5
HALO = 8
GATE_PAD = 128
ROW_SLAB = 16
CTX_ROWS = 8
NORM_COLS = 128

VMEM_LIMIT = 56 * 1024 * 1024


def _sigmoid(x):
    return 1.0 / (1.0 + jnp.exp(-x))


def _log_sigmoid(x):
    return jnp.minimum(x, 0.0) - jnp.log(1.0 + jnp.exp(-jnp.abs(x)))


def _gelu_tanh(x):
    c = 0.7978845608028654
    return 0.5 * x * (1.0 + jnp.tanh(c * (x + 0.044715 * (x * x * x))))


def _rms(x, g):
    return x * lax.rsqrt(jnp.mean(x * x, axis=-1, keepdims=True) + EPS) * g


def _norm_mod(x, g, shift, scale):
    return _rms(x, g) * (1.0 + scale) + shift


def _dot(a, b):
    return jnp.dot(a, b, preferred_element_type=F32)


def _ada_kernel(c_ref, w_ref, b_ref, o_ref):
    c = c_ref[...]
    s = (c * _sigmoid(c)).astype(BF16)
    o_ref[...] = _dot(s, w_ref[...].astype(BF16)) + b_ref[...]


def _adaln(cond, w, b, tn=1024):
    rows, d = cond.shape
    n = w.shape[1]
    return pl.pallas_call(
        _ada_kernel,
        grid=(n // tn,),
        in_specs=[pl.BlockSpec((rows, d), lambda j: (0, 0)),
                  pl.BlockSpec((d, tn), lambda j: (0, j)),
                  pl.BlockSpec((1, tn), lambda j: (0, j))],
        out_specs=pl.BlockSpec((rows, tn), lambda j: (0, j)),
        out_shape=jax.ShapeDtypeStruct((rows, n), F32),
        compiler_params=pltpu.CompilerParams(dimension_semantics=("parallel",),
                                             vmem_limit_bytes=VMEM_LIMIT),
        name="adaln",
    )(cond, w, b.reshape(1, n))


def _ffn_kernel(*refs, mod_base, final):
    if final:
        x_ref, mod_ref, g_ref, wg_ref, wu_ref, wo_ref, fn_ref, o_ref, xn_ref = refs
    else:
        x_ref, mod_ref, g_ref, wg_ref, wu_ref, wo_ref, o_ref, xn_ref = refs
    f = pl.program_id(2)
    tm = xn_ref.shape[0]

    @pl.when(f == 0)
    def _():
        shift = mod_ref[0, mod_base:mod_base + 1, :]
        gain = g_ref[...] * (1.0 + mod_ref[0, mod_base + 1:mod_base + 2, :])

        def slab(i, carry):
            rows = pl.ds(pl.multiple_of(i * ROW_SLAB, ROW_SLAB), ROW_SLAB)
            x = x_ref[0, rows, :]
            r = lax.rsqrt(jnp.mean(x * x, axis=-1, keepdims=True) + EPS)
            xn_ref[rows, :] = (x * r * gain + shift).astype(BF16)
            o_ref[0, rows, :] = jnp.zeros_like(x)
            return carry

        lax.fori_loop(0, tm // ROW_SLAB, slab, 0, unroll=8)

    xn = xn_ref[...]
    gate = _dot(xn, wg_ref[...])
    up = _dot(xn, wu_ref[...])
    h = (gate * _sigmoid(gate) * up).astype(BF16)
    o_ref[0] += _dot(h, wo_ref[...])

    @pl.when(f == pl.num_programs(2) - 1)
    def _():
        def slab(i, carry):
            rows = pl.ds(pl.multiple_of(i * ROW_SLAB, ROW_SLAB), ROW_SLAB)
            y = x_ref[0, rows, :] + 0.5 * mod_ref[0, mod_base + 2:mod_base + 3, :] * o_ref[0, rows, :]
            if final:
                y = _rms(y, fn_ref[...])
            o_ref[0, rows, :] = y
            return carry

        lax.fori_loop(0, tm // ROW_SLAB, slab, 0)


def _ffn(x, mods, mod_row, mod_base, g, w_in, w_out, final_g=None, tm=1024, tf=512):
    bsz, s, d = x.shape
    ff = w_out.shape[0]
    tm = min(tm, s)
    nf = ff // tf
    final = final_g is not None
    in_specs = [pl.BlockSpec((1, tm, d), lambda b, i, f: (b, i, 0)),
                pl.BlockSpec((1, N_MOD, d), lambda b, i, f: (mod_row(b), 0, 0)),
                pl.BlockSpec((1, d), lambda b, i, f: (0, 0)),
                pl.BlockSpec((d, tf), lambda b, i, f: (0, f)),
                pl.BlockSpec((d, tf), lambda b, i, f: (0, nf + f)),
                pl.BlockSpec((tf, d), lambda b, i, f: (f, 0))]
    args = [x, mods, g.reshape(1, d), w_in, w_in, w_out]
    if final:
        in_specs.append(pl.BlockSpec((1, d), lambda b, i, f: (0, 0)))
        args.append(final_g.reshape(1, d))
    return pl.pallas_call(
        functools.partial(_ffn_kernel, mod_base=mod_base, final=final),
        grid=(bsz, s // tm, nf),
        in_specs=in_specs,
        out_specs=pl.BlockSpec((1, tm, d), lambda b, i, f: (b, i, 0)),
        out_shape=jax.ShapeDtypeStruct((bsz, s, d), F32),
        scratch_shapes=[pltpu.VMEM((tm, d), BF16)],
        compiler_params=pltpu.CompilerParams(
            dimension_semantics=("parallel", "parallel", "arbitrary"),
            vmem_limit_bytes=VMEM_LIMIT),
        name="ffn_final" if final else "ffn",
    )(*args)


def _inproj_kernel(*refs, ctx_mode, tm):
    if ctx_mode:
        (x_ref, xp_ref, xq_ref, mod_ref, g_ref, wqk_ref, wv_ref, wg_ref, cw_ref, cb_ref,
         k_ref, v_ref, zg_ref, zs_ref, xn_ref) = refs
    else:
        (x_ref, xp_ref, xq_ref, mod_ref, g_ref, wqk_ref, wv_ref, wg_ref, cw_ref, cb_ref,
         wo_ref, wm_ref, gn_ref,
         q_ref, k_ref, v_ref, zg_ref, o_ref, u_ref, vn_ref, zs_ref, xn_ref) = refs
    i = pl.program_id(1)
    g = g_ref[...]
    shift = mod_ref[0, 3:4, :]
    scale = mod_ref[0, 4:5, :]
    gain = g * (1.0 + scale)

    def slab(t, carry):
        rows = pl.ds(pl.multiple_of(t * ROW_SLAB, ROW_SLAB), ROW_SLAB)
        x = x_ref[0, rows, :]
        r = lax.rsqrt(jnp.mean(x * x, axis=-1, keepdims=True) + EPS)
        xn_ref[rows, :] = (x * r * gain + shift).astype(BF16)
        return carry

    lax.fori_loop(0, tm // ROW_SLAB, slab, 0, unroll=8)
    xn = xn_ref[...]
    wqk = wqk_ref[...]

    if not ctx_mode:
        gm = _gelu_tanh(_dot(xn, wm_ref[...]))
        gw = gm.shape[1] // 2
        u_ref[0] = gm[:, :gw].astype(BF16)
        vn_ref[0] = _rms(gm[:, gw:], gn_ref[...]).astype(BF16)
        o_ref[0] = _sigmoid(_dot(xn, wo_ref[...])).astype(BF16)

    zp = _dot(_norm_mod(xp_ref[0], g, shift, scale).astype(BF16), wqk)
    zq = _dot(_norm_mod(xq_ref[0], g, shift, scale).astype(BF16), wqk)
    zs_ref[0:HALO, :] = jnp.where(i == 0, 0.0, zp)
    zs_ref[HALO + tm:2 * HALO + tm, :] = jnp.where(i == pl.num_programs(1) - 1, 0.0, zq)
    zs_ref[HALO:HALO + tm, :] = _dot(xn, wqk)
    conv = cb_ref[...]
    for j in range(CONV_K):
        off = HALO - CONV_K // 2 + j
        conv = conv + zs_ref[off:off + tm, :] * cw_ref[j:j + 1, :]
    qk = conv * _sigmoid(conv)
    if ctx_mode:
        k = qk
    else:
        half = qk.shape[1] // 2
        q_ref[0] = (qk[:, :half] * (DK ** -0.5)).astype(BF16)
        k = qk[:, half:]
    for h in range(HEADS):
        for cc in range(tm // CHUNK):
            k_ref[0, h, cc] = k[cc * CHUNK:(cc + 1) * CHUNK, h * DK:(h + 1) * DK].T.astype(BF16)

    v_ref[0] = _dot(xn, wv_ref[...]).astype(BF16)
    zg_ref[0] = _dot(xn, wg_ref[...])


def _const_spec(shape):
    nd = len(shape)
    return pl.BlockSpec(shape, lambda b, i: (0,) * nd, pipeline_mode=pl.Buffered(1))


def _inproj(x, mods, mod_row, g, wqk, wv, wg, cw, cb, wo=None, wm=None, gn=None, tm=512):
    bsz, s, d = x.shape
    ctx_mode = wo is None
    tm = min(tm, s)
    nh = s // HALO
    r = tm // HALO
    wq = wqk.shape[1]
    in_specs = [pl.BlockSpec((1, tm, d), lambda b, i: (b, i, 0)),
                pl.BlockSpec((1, HALO, d), lambda b, i: (b, jnp.maximum(i * r - 1, 0), 0)),
                pl.BlockSpec((1, HALO, d), lambda b, i: (b, jnp.minimum((i + 1) * r, nh - 1), 0)),
                pl.BlockSpec((1, N_MOD, d), lambda b, i: (mod_row(b), 0, 0)),
                _const_spec((1, d)), _const_spec(wqk.shape), _const_spec(wv.shape),
                _const_spec(wg.shape), _const_spec(cw.shape), _const_spec(cb.shape)]
    args = [x, x, x, mods, g.reshape(1, d), wqk, wv, wg, cw, cb]

    def tok(width, dtype):
        return (pl.BlockSpec((1, tm, width), lambda b, i: (b, i, 0)),
                jax.ShapeDtypeStruct((bsz, s, width), dtype))

    kt = (pl.BlockSpec((1, HEADS, tm // CHUNK, DK, CHUNK), lambda b, i: (b, 0, i, 0, 0)),
          jax.ShapeDtypeStruct((bsz, HEADS, s // CHUNK, DK, CHUNK), BF16))
    if ctx_mode:
        outs = [kt, tok(wv.shape[1], BF16), tok(GATE_PAD, F32)]
    else:
        in_specs += [_const_spec(wo.shape), _const_spec(wm.shape), _const_spec(gn.shape)]
        args += [wo, wm, gn]
        outs = [tok(wq // 2, BF16), kt, tok(wv.shape[1], BF16), tok(GATE_PAD, F32),
                tok(wo.shape[1], BF16), tok(wm.shape[1] // 2, BF16), tok(wm.shape[1] // 2, BF16)]
    return pl.pallas_call(
        functools.partial(_inproj_kernel, ctx_mode=ctx_mode, tm=tm),
        grid=(bsz, s // tm),
        in_specs=in_specs,
        out_specs=[o[0] for o in outs],
        out_shape=[o[1] for o in outs],
        scratch_shapes=[pltpu.VMEM((tm + 2 * HALO, wq), F32), pltpu.VMEM((tm, d), BF16)],
        compiler_params=pltpu.CompilerParams(dimension_semantics=("parallel", "parallel"),
                                             vmem_limit_bytes=VMEM_LIMIT),
        name="inproj_ctx" if ctx_mode else "inproj",
    )(*args)


def _lane_scan(x, op, ident, reverse):
    lane = lax.broadcasted_iota(jnp.int32, x.shape, 1)
    n = x.shape[1]
    k = 1
    while k < n:
        if reverse:
            x = op(x, jnp.where(lane < n - k, pltpu.roll(x, n - k, axis=1), ident))
        else:
            x = op(x, jnp.where(lane >= k, pltpu.roll(x, k, axis=1), ident))
        k *= 2
    return x


def _mlstm_kernel(q_ref, kt_ref, v_ref, o_ref, g_ref, ktc_ref, vc_ref, br_ref, ng_ref, out_ref,
                  e_ref, col_ref, tot_ref, emax_ref, cs_ref, ms_ref, c_ref, m_ref):
    s = q_ref.shape[1]
    nc = s // CHUNK
    ncc = vc_ref.shape[1] // CHUNK
    row = lax.broadcasted_iota(jnp.int32, (CHUNK, CHUNK), 0)
    col = lax.broadcasted_iota(jnp.int32, (CHUNK, CHUNK), 1)
    eye = row == col
    ones = jnp.ones((CHUNK, NORM_COLS), BF16)
    sel_r = lax.broadcasted_iota(jnp.int32, (3 * CHUNK, 2 * NORM_COLS), 0)
    sel_c = lax.broadcasted_iota(jnp.int32, (3 * CHUNK, 2 * NORM_COLS), 1)
    sel = jnp.where((sel_r < CHUNK) == (sel_c < NORM_COLS), 1.0, 0.0).astype(BF16)

    for d in range(2):
        logi = g_ref[0, 0, 2 * d] + br_ref[0, 2 * d]
        logf = _log_sigmoid(g_ref[0, 0, 2 * d + 1] + br_ref[0, 2 * d + 1])
        b = _lane_scan(logf, jnp.add, 0.0, d == 1)
        e = logi - b
        e_ref[d] = e
        b_hi = b.astype(BF16).astype(F32)
        col_ref[d, 0] = _lane_scan(e, jnp.maximum, -jnp.inf, d == 1).astype(BF16).astype(F32)
        col_ref[d, 1] = b_hi
        col_ref[d, 2] = (b - b_hi).astype(BF16).astype(F32)
        tot_ref[d] = jnp.sum(logf, axis=-1, keepdims=True)
        emax_ref[d] = jnp.max(e, axis=-1, keepdims=True)

    def scan_step(d, r, kt, v):
        e_r = e_ref[d, pl.ds(r, 1), :]
        emax = emax_ref[d, pl.ds(r, 1), :]
        kw = (kt.astype(F32) * jnp.exp(e_r - emax)).astype(BF16)
        upd = _dot(kw, jnp.concatenate([v, ones], axis=1))
        m = m_ref[d]
        c = c_ref[d]
        mx = jnp.maximum(m, emax)
        c_ref[d] = jnp.exp(m - mx) * c + jnp.exp(emax - mx) * upd
        m_ref[d] = tot_ref[d, pl.ds(r, 1), :] + mx
        return c.astype(BF16), m

    c_ref[...] = jnp.zeros_like(c_ref)
    m_ref[...] = jnp.zeros_like(m_ref)
    for t in range(ncc):
        for d in range(2):
            ci = t if d == 0 else ncc - 1 - t
            scan_step(d, ci, ktc_ref[0, 0, ci], vc_ref[0, ci * CHUNK:(ci + 1) * CHUNK, :])

    def scan_body(t, carry):
        for d in range(2):
            ci = t if d == 0 else nc - 1 - t
            rows = pl.ds(pl.multiple_of(ci * CHUNK, CHUNK), CHUNK)
            c_in, m_in = scan_step(d, CTX_ROWS + ci, kt_ref[0, 0, ci], v_ref[0, rows, :])
            cs_ref[d, ci] = c_in
            ms_ref[d, ci] = m_in
        return carry

    lax.fori_loop(0, nc, scan_body, 0, unroll=4)

    def out_body(ci, carry):
        rows = pl.ds(pl.multiple_of(ci * CHUNK, CHUNK), CHUNK)
        q = q_ref[0, rows, :]
        qf = q.astype(F32)
        v_aug = jnp.concatenate([v_ref[0, rows, :], ones], axis=1)
        qk = _dot(q, kt_ref[0, 0, ci])
        h = None
        for d in range(2):
            r = pl.ds(CTX_ROWS + ci, 1)
            inc = (col <= row) if d == 0 else (col >= row)
            e_m = jnp.where(inc, e_ref[d, r, :], -jnp.inf)
            diag = jnp.concatenate([jnp.where(eye, col_ref[d, i, r, :], 0.0) for i in range(3)], axis=1)
            rep = _dot(diag.astype(BF16), sel)
            m = ms_ref[d, ci]
            mm = jnp.maximum(m, rep[:, :NORM_COLS])
            b_q = rep[:, NORM_COLS:]
            p = (qk * jnp.exp(e_m - mm)).astype(BF16)
            qa = (qf * jnp.exp(m - mm)).astype(BF16)
            comb = _dot(qa, cs_ref[d, ci]) + _dot(p, v_aug)
            inv = 1.0 / jnp.maximum(jnp.abs(comb[:, DV:]), jnp.exp(-(b_q + mm)))
            hd = comb[:, :DV] * jnp.concatenate([inv] * (DV // NORM_COLS), axis=1)
            h = hd if h is None else h + hd
        out_ref[0, rows, :] = (_rms(h, ng_ref[...]) * o_ref[0, rows, :].astype(F32)).astype(BF16)
        return carry

    lax.fori_loop(0, nc, out_body, 0, unroll=2)


def _mlstm(q, kt, v, o, g, ktc, vc, bias, norm_g):
    bsz, s, _ = q.shape
    sc = vc.shape[1]
    nc, ncc = s // CHUNK, sc // CHUNK
    nr = g.shape[3]
    return pl.pallas_call(
        _mlstm_kernel,
        grid=(bsz, HEADS),
        in_specs=[pl.BlockSpec((1, s, DK), lambda b, h: (b, 0, h)),
                  pl.BlockSpec((1, 1, nc, DK, CHUNK), lambda b, h: (b, h, 0, 0, 0)),
                  pl.BlockSpec((1, s, DV), lambda b, h: (b, 0, h)),
                  pl.BlockSpec((1, s, DV), lambda b, h: (b, 0, h)),
                  pl.BlockSpec((1, 1, 4, nr, CHUNK), lambda b, h: (b, h, 0, 0, 0)),
                  pl.BlockSpec((1, 1, ncc, DK, CHUNK), lambda b, h: (b, h, 0, 0, 0)),
                  pl.BlockSpec((1, sc, DV), lambda b, h: (b, 0, h)),
                  pl.BlockSpec((1, 4, 1, 1), lambda b, h: (h, 0, 0, 0)),
                  pl.BlockSpec((1, DV), lambda b, h: (0, h))],
        out_specs=pl.BlockSpec((1, s, DV), lambda b, h: (b, 0, h)),
        out_shape=jax.ShapeDtypeStruct((bsz, s, HEADS * DV), BF16),
        scratch_shapes=[pltpu.VMEM((2, nr, CHUNK), F32), pltpu.VMEM((2, 3, nr, CHUNK), F32),
                        pltpu.VMEM((2, nr, 1), F32), pltpu.VMEM((2, nr, 1), F32),
                        pltpu.VMEM((2, nc, DK, DV + NORM_COLS), BF16), pltpu.VMEM((2, nc, 1, 1), F32),
                        pltpu.VMEM((2, DK, DV + NORM_COLS), F32), pltpu.VMEM((2, 1, 1), F32)],
        compiler_params=pltpu.CompilerParams(dimension_semantics=("parallel", "parallel"),
                                             vmem_limit_bytes=VMEM_LIMIT),
        name="mlstm",
    )(q, kt, v, o, g, ktc, vc, bias, norm_g)


def _outproj_kernel(x_ref, mod_ref, hx_ref, u_ref, vn_ref, gw_ref, gb_ref, wt_ref, wb_ref, o_ref, gx_ref):
    tm = x_ref.shape[1]
    for ci in range(tm // CHUNK):
        rows = slice(ci * CHUNK, (ci + 1) * CHUNK)
        for g in range(GROUPS):
            cols = slice(g * GD, (g + 1) * GD)
            sp = _dot(gw_ref[g], vn_ref[0, rows, cols]) + gb_ref[:, cols]
            gx_ref[rows, cols] = (u_ref[0, rows, cols].astype(F32) * sp).astype(BF16)
    y = _dot(hx_ref[0], wt_ref[...]) + _dot(gx_ref[...], wb_ref[...])
    o_ref[0] = x_ref[0] + mod_ref[0, 5:6, :] * y


def _outproj(x, mods, hx, u, vn, gw, gb, wt, wb, tm=512):
    bsz, s, d = x.shape
    w = hx.shape[2]
    return pl.pallas_call(
        _outproj_kernel,
        grid=(bsz, s // tm),
        in_specs=[pl.BlockSpec((1, tm, d), lambda b, i: (b, i, 0)),
                  pl.BlockSpec((1, N_MOD, d), lambda b, i: (b, 0, 0)),
                  pl.BlockSpec((1, tm, w), lambda b, i: (b, i, 0)),
                  pl.BlockSpec((1, tm, w), lambda b, i: (b, i, 0)),
                  pl.BlockSpec((1, tm, w), lambda b, i: (b, i, 0)),
                  _const_spec(gw.shape), _const_spec(gb.shape),
                  _const_spec(wt.shape), _const_spec(wb.shape)],
        out_specs=pl.BlockSpec((1, tm, d), lambda b, i: (b, i, 0)),
        out_shape=jax.ShapeDtypeStruct((bsz, s, d), F32),
        scratch_shapes=[pltpu.VMEM((tm, w), BF16)],
        compiler_params=pltpu.CompilerParams(dimension_semantics=("parallel", "parallel"),
                                             vmem_limit_bytes=VMEM_LIMIT),
        name="outproj",
    )(x, mods, hx, u, vn, gw, gb, wt, wb)


def _gate_rows(zgc, zg):
    def rows(z):
        bsz, t, _ = z.shape
        g = z[:, :, :4 * HEADS].reshape(bsz, t // CHUNK, CHUNK, 4, HEADS)
        return g.transpose(0, 4, 3, 1, 2)
    gc = rows(zgc)
    assert gc.shape[3] <= CTX_ROWS
    gc = jnp.pad(gc, ((0, 0), (0, 0), (0, 0), (0, CTX_ROWS - gc.shape[3]), (0, 0)))
    return jnp.concatenate([gc, rows(zg)], axis=3)


def kernel(x, c, ctx, c_ctx, w_ada, b_ada, norm_ffn1, w_ffn1_in, w_ffn1_out, norm_mix, w_in, conv_w, conv_b, b_igate, b_fgate, mlstm_norm, gmlp_norm, gmlp_w, gmlp_b, w_out, norm_ffn2, w_ffn2_in, w_ffn2_out, final_norm):
    bsz, s, d = x.shape
    depth = w_ada.shape[0]
    assert depth == 1, "context-stream update for non-final layers is not implemented"
    l = 0
    qk_w = 2 * HEADS * DK
    mw = HEADS * DV
    c_qk, c_v, c_o = qk_w, qk_w + mw, qk_w + 2 * mw
    c_gate = c_o + 4 * HEADS

    lat_row = lambda b: b
    ctx_row = lambda b: bsz

    rows = -(-(bsz + 1) // 8) * 8
    cond = jnp.zeros((rows, d), F32).at[:bsz].set(c).at[bsz].set(c_ctx)
    mods = _adaln(cond, w_ada[l], b_ada[l]).reshape(rows, N_MOD, d)

    w1i, w1o = w_ffn1_in[l].astype(BF16), w_ffn1_out[l].astype(BF16)
    w2i, w2o = w_ffn2_in[l].astype(BF16), w_ffn2_out[l].astype(BF16)
    wi = w_in[l].astype(BF16)
    wqk, wv, wo = wi[:, :c_qk], wi[:, c_qk:c_v], wi[:, c_v:c_o]
    wg = jnp.pad(wi[:, c_o:c_gate], ((0, 0), (0, GATE_PAD - 4 * HEADS)))
    wm = wi[:, c_gate:]
    cw = jnp.pad(conv_w[l], ((0, 8 - CONV_K), (0, 0)))
    cb = conv_b[l].reshape(1, qk_w)
    wout = w_out[l].astype(BF16)

    x = _ffn(x, mods, lat_row, 0, norm_ffn1[l], w1i, w1o)
    ctx = _ffn(ctx.reshape(1, -1, d), mods, ctx_row, 0, norm_ffn1[l], w1i, w1o).reshape(ctx.shape)

    q, kt, v, zg, o, u, vn = _inproj(x, mods, lat_row, norm_mix[l], wqk, wv, wg, cw, cb,
                                     wo, wm, gmlp_norm[l].reshape(1, -1))
    ktc, vc, zgc = _inproj(ctx, mods, ctx_row, norm_mix[l], wqk[:, qk_w // 2:], wv, wg,
                           cw[:, qk_w // 2:], cb[:, qk_w // 2:])

    bias = jnp.stack([b_igate[l][0], b_fgate[l][0], b_igate[l][1], b_fgate[l][1]], axis=-1)
    hx = _mlstm(q, kt, v, o, _gate_rows(zgc, zg), ktc, vc, bias.reshape(HEADS, 4, 1, 1),
                mlstm_norm[l].reshape(1, mw))

    gb = jnp.repeat(gmlp_b[l].T, GD, axis=1)
    x = _outproj(x, mods, hx, u, vn, gmlp_w[l].astype(BF16), gb, wout[:mw], wout[mw:])

    return _ffn(x, mods, lat_row, 6, norm_ffn2[l], w2i, w2o, final_g=final_norm)
```

```python
import functools

import jax
import jax.numpy as jnp
from jax import lax
from jax.experimental import pallas as pl
from jax.experimental.pallas import tpu as pltpu

F32 = jnp.float32
BF16 = jnp.bfloat16

EPS = 1e-6
N_MOD = 9
HEADS = 4
DK = 128
DV = 256
CHUNK = 128
GROUPS = 8
GD = 128
CONV_K = 5
HALO = 8
GATE_PAD = 128
GATE_ROWS = 4 * HEADS
ROW_SLAB = 16
CTX_ROWS = 8
NORM_COLS = 128

VMEM_LIMIT = 56 * 1024 * 1024


def _sigmoid(x):
    return 1.0 / (1.0 + jnp.exp(-x))


def _log_sigmoid(x):
    return jnp.minimum(x, 0.0) - jnp.log(1.0 + jnp.exp(-jnp.abs(x)))


def _gelu_tanh(x):
    c = 0.7978845608028654
    return 0.5 * x * (1.0 + jnp.tanh(c * (x + 0.044715 * (x * x * x))))


def _rms(x, g):
    return x * lax.rsqrt(jnp.mean(x * x, axis=-1, keepdims=True) + EPS) * g


def _norm_mod(x, g, shift, scale):
    return _rms(x, g) * (1.0 + scale) + shift


def _dot(a, b):
    return jnp.dot(a, b, preferred_element_type=F32)


def _ada_kernel(c_ref, w_ref, b_ref, o_ref):
    c = c_ref[...]
    s = (c * _sigmoid(c)).astype(BF16)
    o_ref[...] = _dot(s, w_ref[...].astype(BF16)) + b_ref[...]


def _adaln(cond, w, b, tn=1024):
    rows, d = cond.shape
    n = w.shape[1]
    return pl.pallas_call(
        _ada_kernel,
        grid=(n // tn,),
        in_specs=[pl.BlockSpec((rows, d), lambda j: (0, 0)),
                  pl.BlockSpec((d, tn), lambda j: (0, j)),
                  pl.BlockSpec((1, tn), lambda j: (0, j))],
        out_specs=pl.BlockSpec((rows, tn), lambda j: (0, j)),
        out_shape=jax.ShapeDtypeStruct((rows, n), F32),
        compiler_params=pltpu.CompilerParams(dimension_semantics=("parallel",),
                                             vmem_limit_bytes=VMEM_LIMIT),
        name="adaln",
    )(cond, w, b.reshape(1, n))


def _ffn_kernel(*refs, mod_base, final):
    if final:
        x_ref, mod_ref, g_ref, wg_ref, wu_ref, wo_ref, fn_ref, o_ref, xn_ref = refs
    else:
        x_ref, mod_ref, g_ref, wg_ref, wu_ref, wo_ref, o_ref, xn_ref = refs
    f = pl.program_id(2)
    tm = xn_ref.shape[0]

    @pl.when(f == 0)
    def _():
        shift = mod_ref[0, mod_base:mod_base + 1, :]
        gain = g_ref[...] * (1.0 + mod_ref[0, mod_base + 1:mod_base + 2, :])

        def slab(i, carry):
            rows = pl.ds(pl.multiple_of(i * ROW_SLAB, ROW_SLAB), ROW_SLAB)
            x = x_ref[0, rows, :]
            r = lax.rsqrt(jnp.mean(x * x, axis=-1, keepdims=True) + EPS)
            xn_ref[rows, :] = (x * r * gain + shift).astype(BF16)
            o_ref[0, rows, :] = jnp.zeros_like(x)
            return carry

        lax.fori_loop(0, tm // ROW_SLAB, slab, 0, unroll=8)

    xn = xn_ref[...]
    gate = _dot(xn, wg_ref[...])
    up = _dot(xn, wu_ref[...])
    h = (gate * _sigmoid(gate) * up).astype(BF16)
    o_ref[0] += _dot(h, wo_ref[...])

    @pl.when(f == pl.num_programs(2) - 1)
    def _():
        def slab(i, carry):
            rows = pl.ds(pl.multiple_of(i * ROW_SLAB, ROW_SLAB), ROW_SLAB)
            y = x_ref[0, rows, :] + 0.5 * mod_ref[0, mod_base + 2:mod_base + 3, :] * o_ref[0, rows, :]
            if final:
                y = _rms(y, fn_ref[...])
            o_ref[0, rows, :] = y
            return carry

        lax.fori_loop(0, tm // ROW_SLAB, slab, 0, unroll=8)


def _ffn(x, mods, mod_row, mod_base, g, w_in, w_out, final_g=None, tm=1024, tf=512):
    bsz, s, d = x.shape
    ff = w_out.shape[0]
    tm = min(tm, s)
    nf = ff // tf
    final = final_g is not None
    in_specs = [pl.BlockSpec((1, tm, d), lambda b, i, f: (b, i, 0)),
                pl.BlockSpec((1, N_MOD, d), lambda b, i, f: (mod_row(b), 0, 0)),
                pl.BlockSpec((1, d), lambda b, i, f: (0, 0)),
                pl.BlockSpec((d, tf), lambda b, i, f: (0, f)),
                pl.BlockSpec((d, tf), lambda b, i, f: (0, nf + f)),
                pl.BlockSpec((tf, d), lambda b, i, f: (f, 0))]
    args = [x, mods, g.reshape(1, d), w_in, w_in, w_out]
    if final:
        in_specs.append(pl.BlockSpec((1, d), lambda b, i, f: (0, 0)))
        args.append(final_g.reshape(1, d))
    return pl.pallas_call(
        functools.partial(_ffn_kernel, mod_base=mod_base, final=final),
        grid=(bsz, s // tm, nf),
        in_specs=in_specs,
        out_specs=pl.BlockSpec((1, tm, d), lambda b, i, f: (b, i, 0)),
        out_shape=jax.ShapeDtypeStruct((bsz, s, d), F32),
        scratch_shapes=[pltpu.VMEM((tm, d), BF16)],
        compiler_params=pltpu.CompilerParams(
            dimension_semantics=("parallel", "parallel", "arbitrary"),
            vmem_limit_bytes=VMEM_LIMIT),
        name="ffn_final" if final else "ffn",
    )(*args)


def _inproj_kernel(*refs, ctx_mode, tm):
    if ctx_mode:
        (x_ref, xp_ref, xq_ref, mod_ref, g_ref, wqk_ref, wv_ref, wg_ref, cw_ref, cb_ref,
         k_ref, v_ref, zg_ref, zs_ref, xn_ref, zgs_ref) = refs
    else:
        (x_ref, xp_ref, xq_ref, mod_ref, g_ref, wqk_ref, wv_ref, wg_ref, cw_ref, cb_ref,
         wo_ref, wm_ref, gn_ref,
         q_ref, k_ref, v_ref, zg_ref, o_ref, u_ref, vn_ref, zs_ref, xn_ref, zgs_ref) = refs
    i = pl.program_id(1)
    g = g_ref[...]
    shift = mod_ref[0, 3:4, :]
    scale = mod_ref[0, 4:5, :]
    gain = g * (1.0 + scale)

    def slab(t, carry):
        rows = pl.ds(pl.multiple_of(t * ROW_SLAB, ROW_SLAB), ROW_SLAB)
        x = x_ref[0, rows, :]
        r = lax.rsqrt(jnp.mean(x * x, axis=-1, keepdims=True) + EPS)
        xn_ref[rows, :] = (x * r * gain + shift).astype(BF16)
        return carry

    lax.fori_loop(0, tm // ROW_SLAB, slab, 0, unroll=8)
    xn = xn_ref[...]
    wqk = wqk_ref[...]

    if not ctx_mode:
        gm = _gelu_tanh(_dot(xn, wm_ref[...]))
        gw = gm.shape[1] // 2
        u_ref[0] = gm[:, :gw].astype(BF16)
        vn_ref[0] = _rms(gm[:, gw:], gn_ref[...]).astype(BF16)
        o_ref[0] = _sigmoid(_dot(xn, wo_ref[...])).astype(BF16)

    zp = _dot(_norm_mod(xp_ref[0], g, shift, scale).astype(BF16), wqk)
    zq = _dot(_norm_mod(xq_ref[0], g, shift, scale).astype(BF16), wqk)
    zs_ref[0:HALO, :] = jnp.where(i == 0, 0.0, zp)
    zs_ref[HALO + tm:2 * HALO + tm, :] = jnp.where(i == pl.num_programs(1) - 1, 0.0, zq)
    zs_ref[HALO:HALO + tm, :] = _dot(xn, wqk)
    conv = cb_ref[...]
    for j in range(CONV_K):
        off = HALO - CONV_K // 2 + j
        conv = conv + zs_ref[off:off + tm, :] * cw_ref[j:j + 1, :]
    qk = conv * _sigmoid(conv)
    if ctx_mode:
        k = qk
    else:
        half = qk.shape[1] // 2
        q_ref[0] = (qk[:, :half] * (DK ** -0.5)).astype(BF16)
        k = qk[:, half:]
    for h in range(HEADS):
        for cc in range(tm // CHUNK):
            k_ref[0, h, cc] = k[cc * CHUNK:(cc + 1) * CHUNK, h * DK:(h + 1) * DK].T.astype(BF16)

    v_ref[0] = _dot(xn, wv_ref[...]).astype(BF16)
    zgs_ref[...] = _dot(xn, wg_ref[...])
    zg_ref[0] = zgs_ref[...].T[:GATE_ROWS, :]


def _const_spec(shape):
    nd = len(shape)
    return pl.BlockSpec(shape, lambda b, i: (0,) * nd, pipeline_mode=pl.Buffered(1))


def _inproj(x, mods, mod_row, g, wqk, wv, wg, cw, cb, wo=None, wm=None, gn=None, tm=512):
    bsz, s, d = x.shape
    ctx_mode = wo is None
    tm = min(tm, s)
    nh = s // HALO
    r = tm // HALO
    wq = wqk.shape[1]
    in_specs = [pl.BlockSpec((1, tm, d), lambda b, i: (b, i, 0)),
                pl.BlockSpec((1, HALO, d), lambda b, i: (b, jnp.maximum(i * r - 1, 0), 0)),
                pl.BlockSpec((1, HALO, d), lambda b, i: (b, jnp.minimum((i + 1) * r, nh - 1), 0)),
                pl.BlockSpec((1, N_MOD, d), lambda b, i: (mod_row(b), 0, 0)),
                _const_spec((1, d)), _const_spec(wqk.shape), _const_spec(wv.shape),
                _const_spec(wg.shape), _const_spec(cw.shape), _const_spec(cb.shape)]
    args = [x, x, x, mods, g.reshape(1, d), wqk, wv, wg, cw, cb]

    def tok(width, dtype):
        return (pl.BlockSpec((1, tm, width), lambda b, i: (b, i, 0)),
                jax.ShapeDtypeStruct((bsz, s, width), dtype))

    kt = (pl.BlockSpec((1, HEADS, tm // CHUNK, DK, CHUNK), lambda b, i: (b, 0, i, 0, 0)),
          jax.ShapeDtypeStruct((bsz, HEADS, s // CHUNK, DK, CHUNK), BF16))
    zgt = (pl.BlockSpec((1, GATE_ROWS, tm), lambda b, i: (b, 0, i)),
           jax.ShapeDtypeStruct((bsz, GATE_ROWS, s), F32))
    if ctx_mode:
        outs = [kt, tok(wv.shape[1], BF16), zgt]
    else:
        in_specs += [_const_spec(wo.shape), _const_spec(wm.shape), _const_spec(gn.shape)]
        args += [wo, wm, gn]
        outs = [tok(wq // 2, BF16), kt, tok(wv.shape[1], BF16), zgt,
                tok(wo.shape[1], BF16), tok(wm.shape[1] // 2, BF16), tok(wm.shape[1] // 2, BF16)]
    return pl.pallas_call(
        functools.partial(_inproj_kernel, ctx_mode=ctx_mode, tm=tm),
        grid=(bsz, s // tm),
        in_specs=in_specs,
        out_specs=[o[0] for o in outs],
        out_shape=[o[1] for o in outs],
        scratch_shapes=[pltpu.VMEM((tm + 2 * HALO, wq), F32), pltpu.VMEM((tm, d), BF16),
                        pltpu.VMEM((tm, GATE_PAD), F32)],
        compiler_params=pltpu.CompilerParams(dimension_semantics=("parallel", "parallel"),
                                             vmem_limit_bytes=VMEM_LIMIT),
        name="inproj_ctx" if ctx_mode else "inproj",
    )(*args)


def _lane_scan(x, op, ident, reverse):
    lane = lax.broadcasted_iota(jnp.int32, x.shape, 1)
    n = x.shape[1]
    k = 1
    while k < n:
        if reverse:
            x = op(x, jnp.where(lane < n - k, pltpu.roll(x, n - k, axis=1), ident))
        else:
            x = op(x, jnp.where(lane >= k, pltpu.roll(x, k, axis=1), ident))
        k *= 2
    return x


def _mlstm_kernel(q_ref, kt_ref, v_ref, o_ref, g_ref, ktc_ref, vc_ref, br_ref, ng_ref, out_ref,
                  e_ref, col_ref, tot_ref, emax_ref, rhs_ref, ms_ref, c_ref, m_ref, lhs_ref, floor_ref):
    s = q_ref.shape[1]
    nc = s // CHUNK
    ncc = vc_ref.shape[1] // CHUNK
    row = lax.broadcasted_iota(jnp.int32, (CHUNK, CHUNK), 0)
    col = lax.broadcasted_iota(jnp.int32, (CHUNK, CHUNK), 1)
    eye = row == col
    ones = jnp.ones((CHUNK, NORM_COLS), BF16)
    sel_r = lax.broadcasted_iota(jnp.int32, (3 * CHUNK, 2 * NORM_COLS), 0)
    sel_c = lax.broadcasted_iota(jnp.int32, (3 * CHUNK, 2 * NORM_COLS), 1)
    sel = jnp.where((sel_r < CHUNK) == (sel_c < NORM_COLS), 1.0, 0.0).astype(BF16)

    for d in range(2):
        logi = g_ref[0, 2 * d, 0] + br_ref[0, 2 * d]
        logf = _log_sigmoid(g_ref[0, 2 * d + 1, 0] + br_ref[0, 2 * d + 1])
        b = _lane_scan(logf, jnp.add, 0.0, d == 1)
        e = logi - b
        e_ref[d] = e
        b_hi = b.astype(BF16).astype(F32)
        col_ref[d, 0] = _lane_scan(e, jnp.maximum, -jnp.inf, d == 1).astype(BF16).astype(F32)
        col_ref[d, 1] = b_hi
        col_ref[d, 2] = (b - b_hi).astype(BF16).astype(F32)
        tot_ref[d] = jnp.sum(logf, axis=-1, keepdims=True)
        emax_ref[d] = jnp.max(e, axis=-1, keepdims=True)

    def scan_step(d, r, kt, v):
        e_r = e_ref[d, pl.ds(r, 1), :]
        emax = emax_ref[d, pl.ds(r, 1), :]
        kw = (kt.astype(F32) * jnp.exp(e_r - emax)).astype(BF16)
        v_aug = jnp.concatenate([v, ones], axis=1)
        upd = _dot(kw, v_aug)
        m = m_ref[d]
        c = c_ref[d]
        mx = jnp.maximum(m, emax)
        c_ref[d] = jnp.exp(m - mx) * c + jnp.exp(emax - mx) * upd
        m_ref[d] = tot_ref[d, pl.ds(r, 1), :] + mx
        return c.astype(BF16), m, v_aug

    c_ref[...] = jnp.zeros_like(c_ref)
    m_ref[...] = jnp.zeros_like(m_ref)
    for t in range(ncc):
        for d in range(2):
            ci = t if d == 0 else ncc - 1 - t
            scan_step(d, ci, ktc_ref[0, 0, ci], vc_ref[0, ci * CHUNK:(ci + 1) * CHUNK, :])

    def scan_body(t, carry):
        for d in range(2):
            ci = t if d == 0 else nc - 1 - t
            rows = pl.ds(pl.multiple_of(ci * CHUNK, CHUNK), CHUNK)
            c_in, m_in, v_aug = scan_step(d, CTX_ROWS + ci, kt_ref[0, 0, ci], v_ref[0, rows, :])
            rhs_ref[d, ci, 0:DK, :] = c_in
            rhs_ref[d, ci, DK:, :] = v_aug
            ms_ref[d, ci] = m_in
        return carry

    lax.fori_loop(0, nc, scan_body, 0, unroll=4)

    def weights(ci, slot):
        q = q_ref[0, pl.ds(pl.multiple_of(ci * CHUNK, CHUNK), CHUNK), :]
        qf = q.astype(F32)
        qk = _dot(q, kt_ref[0, 0, ci])
        r = pl.ds(CTX_ROWS + ci, 1)
        for d in range(2):
            inc = (col <= row) if d == 0 else (col >= row)
            e_m = jnp.where(inc, e_ref[d, r, :], -jnp.inf)
            diag = jnp.concatenate([jnp.where(eye, col_ref[d, i, r, :], 0.0) for i in range(3)], axis=1)
            rep = _dot(diag.astype(BF16), sel)
            m = ms_ref[d, ci]
            mm = jnp.maximum(m, rep[:, :NORM_COLS])
            p = (qk * jnp.exp(e_m - mm)).astype(BF16)
            qa = (qf * jnp.exp(m - mm)).astype(BF16)
            lhs_ref[slot, d] = jnp.concatenate([qa, p], axis=1)
            floor_ref[slot, d] = jnp.exp(-(rep[:, NORM_COLS:] + mm))

    def apply(ci, slot):
        rows = pl.ds(pl.multiple_of(ci * CHUNK, CHUNK), CHUNK)
        h = None
        for d in range(2):
            comb = _dot(lhs_ref[slot, d], rhs_ref[d, ci])
            inv = 1.0 / jnp.maximum(jnp.abs(comb[:, DV:]), floor_ref[slot, d])
            hd = comb[:, :DV] * jnp.concatenate([inv] * (DV // NORM_COLS), axis=1)
            h = hd if h is None else h + hd
        out_ref[0, rows, :] = (_rms(h, ng_ref[...]) * o_ref[0, rows, :].astype(F32)).astype(BF16)

    def out_body(ci, carry):
        apply(ci - 1, (ci - 1) % 2)
        weights(ci, ci % 2)
        return carry

    weights(0, 0)
    lax.fori_loop(1, nc, out_body, 0, unroll=4)
    apply(nc - 1, (nc - 1) % 2)


def _mlstm(q, kt, v, o, g, ktc, vc, bias, norm_g):
    bsz, s, _ = q.shape
    sc = vc.shape[1]
    nc, ncc = s // CHUNK, sc // CHUNK
    nr = g.shape[3]
    return pl.pallas_call(
        _mlstm_kernel,
        grid=(bsz, HEADS),
        in_specs=[pl.BlockSpec((1, s, DK), lambda b, h: (b, 0, h)),
                  pl.BlockSpec((1, 1, nc, DK, CHUNK), lambda b, h: (b, h, 0, 0, 0)),
                  pl.BlockSpec((1, s, DV), lambda b, h: (b, 0, h)),
                  pl.BlockSpec((1, s, DV), lambda b, h: (b, 0, h)),
                  pl.BlockSpec((1, 4, 1, nr, CHUNK), lambda b, h: (b, 0, h, 0, 0)),
                  pl.BlockSpec((1, 1, ncc, DK, CHUNK), lambda b, h: (b, h, 0, 0, 0)),
                  pl.BlockSpec((1, sc, DV), lambda b, h: (b, 0, h)),
                  pl.BlockSpec((1, 4, 1, 1), lambda b, h: (h, 0, 0, 0)),
                  pl.BlockSpec((1, DV), lambda b, h: (0, h))],
        out_specs=pl.BlockSpec((1, s, DV), lambda b, h: (b, 0, h)),
        out_shape=jax.ShapeDtypeStruct((bsz, s, HEADS * DV), BF16),
        scratch_shapes=[pltpu.VMEM((2, nr, CHUNK), F32), pltpu.VMEM((2, 3, nr, CHUNK), F32),
                        pltpu.VMEM((2, nr, 1), F32), pltpu.VMEM((2, nr, 1), F32),
                        pltpu.VMEM((2, nc, DK + CHUNK, DV + NORM_COLS), BF16), pltpu.VMEM((2, nc, 1, 1), F32),
                        pltpu.VMEM((2, DK, DV + NORM_COLS), F32), pltpu.VMEM((2, 1, 1), F32),
                        pltpu.VMEM((2, 2, CHUNK, DK + CHUNK), BF16), pltpu.VMEM((2, 2, CHUNK, NORM_COLS), F32)],
        compiler_params=pltpu.CompilerParams(dimension_semantics=("parallel", "parallel"),
                                             vmem_limit_bytes=VMEM_LIMIT),
        name="mlstm",
    )(q, kt, v, o, g, ktc, vc, bias, norm_g)


def _outproj_kernel(x_ref, mod_ref, hx_ref, u_ref, vn_ref, gw_ref, gb_ref, wt_ref, wb_ref, o_ref, gx_ref):
    tm = x_ref.shape[1]
    for ci in range(tm // CHUNK):
        rows = slice(ci * CHUNK, (ci + 1) * CHUNK)
        for g in range(GROUPS):
            cols = slice(g * GD, (g + 1) * GD)
            sp = _dot(gw_ref[g], vn_ref[0, rows, cols]) + gb_ref[:, cols]
            gx_ref[rows, cols] = (u_ref[0, rows, cols].astype(F32) * sp).astype(BF16)
    y = _dot(hx_ref[0], wt_ref[...]) + _dot(gx_ref[...], wb_ref[...])
    o_ref[0] = x_ref[0] + mod_ref[0, 5:6, :] * y


def _outproj(x, mods, hx, u, vn, gw, gb, wt, wb, tm=512):
    bsz, s, d = x.shape
    w = hx.shape[2]
    return pl.pallas_call(
        _outproj_kernel,
        grid=(bsz, s // tm),
        in_specs=[pl.BlockSpec((1, tm, d), lambda b, i: (b, i, 0)),
                  pl.BlockSpec((1, N_MOD, d), lambda b, i: (b, 0, 0)),
                  pl.BlockSpec((1, tm, w), lambda b, i: (b, i, 0)),
                  pl.BlockSpec((1, tm, w), lambda b, i: (b, i, 0)),
                  pl.BlockSpec((1, tm, w), lambda b, i: (b, i, 0)),
                  _const_spec(gw.shape), _const_spec(gb.shape),
                  _const_spec(wt.shape), _const_spec(wb.shape)],
        out_specs=pl.BlockSpec((1, tm, d), lambda b, i: (b, i, 0)),
        out_shape=jax.ShapeDtypeStruct((bsz, s, d), F32),
        scratch_shapes=[pltpu.VMEM((tm, w), BF16)],
        compiler_params=pltpu.CompilerParams(dimension_semantics=("parallel", "parallel"),
                                             vmem_limit_bytes=VMEM_LIMIT),
        name="outproj",
    )(x, mods, hx, u, vn, gw, gb, wt, wb)


def _gate_rows(zgc, zg):
    def rows(z):
        bsz, _, t = z.shape
        return z.reshape(bsz, 4, HEADS, t // CHUNK, CHUNK)
    gc = rows(zgc)
    assert gc.shape[3] <= CTX_ROWS
    gc = jnp.pad(gc, ((0, 0), (0, 0), (0, 0), (0, CTX_ROWS - gc.shape[3]), (0, 0)))
    return jnp.concatenate([gc, rows(zg)], axis=3)


def kernel(x, c, ctx, c_ctx, w_ada, b_ada, norm_ffn1, w_ffn1_in, w_ffn1_out, norm_mix, w_in, conv_w, conv_b, b_igate, b_fgate, mlstm_norm, gmlp_norm, gmlp_w, gmlp_b, w_out, norm_ffn2, w_ffn2_in, w_ffn2_out, final_norm):
    bsz, s, d = x.shape
    depth = w_ada.shape[0]
    assert depth == 1, "context-stream update for non-final layers is not implemented"
    l = 0
    qk_w = 2 * HEADS * DK
    mw = HEADS * DV
    c_qk, c_v, c_o = qk_w, qk_w + mw, qk_w + 2 * mw
    c_gate = c_o + 4 * HEADS

    lat_row = lambda b: b
    ctx_row = lambda b: bsz

    rows = -(-(bsz + 1) // 8) * 8
    cond = jnp.zeros((rows, d), F32).at[:bsz].set(c).at[bsz].set(c_ctx)
    mods = _adaln(cond, w_ada[l], b_ada[l]).reshape(rows, N_MOD, d)

    w1i, w1o = w_ffn1_in[l].astype(BF16), w_ffn1_out[l].astype(BF16)
    w2i, w2o = w_ffn2_in[l].astype(BF16), w_ffn2_out[l].astype(BF16)
    wi = w_in[l].astype(BF16)
    wqk, wv, wo = wi[:, :c_qk], wi[:, c_qk:c_v], wi[:, c_v:c_o]
    wg = jnp.pad(wi[:, c_o:c_gate], ((0, 0), (0, GATE_PAD - 4 * HEADS)))
    wm = wi[:, c_gate:]
    cw = jnp.pad(conv_w[l], ((0, 8 - CONV_K), (0, 0)))
    cb = conv_b[l].reshape(1, qk_w)
    wout = w_out[l].astype(BF16)

    x = _ffn(x, mods, lat_row, 0, norm_ffn1[l], w1i, w1o)
    ctx = _ffn(ctx.reshape(1, -1, d), mods, ctx_row, 0, norm_ffn1[l], w1i, w1o).reshape(ctx.shape)

    q, kt, v, zg, o, u, vn = _inproj(x, mods, lat_row, norm_mix[l], wqk, wv, wg, cw, cb,
                                     wo, wm, gmlp_norm[l].reshape(1, -1))
    ktc, vc, zgc = _inproj(ctx, mods, ctx_row, norm_mix[l], wqk[:, qk_w // 2:], wv, wg,
                           cw[:, qk_w // 2:], cb[:, qk_w // 2:])

    bias = jnp.stack([b_igate[l][0], b_fgate[l][0], b_igate[l][1], b_fgate[l][1]], axis=-1)
    hx = _mlstm(q, kt, v, o, _gate_rows(zgc, zg), ktc, vc, bias.reshape(HEADS, 4, 1, 1),
                mlstm_norm[l].reshape(1, mw))

    gb = jnp.repeat(gmlp_b[l].T, GD, axis=1)
    x = _outproj(x, mods, hx, u, vn, gmlp_w[l].astype(BF16), gb, wout[:mw], wout[mw:])

    return _ffn(x, mods, lat_row, 6, norm_ffn2[l], w2i, w2o, final_g=final_norm)
```

```python
import functools

import jax
import jax.numpy as jnp
from jax import lax
from jax.experimental import pallas as pl
from jax.experimental.pallas import tpu as pltpu

F32 = jnp.float32
BF16 = jnp.bfloat16

EPS = 1e-6
N_MOD = 9
HEADS = 4
DK = 128
DV = 256
CHUNK = 128
GROUPS = 8
GD = 128
CONV_K = 5
HALO = 16
COL_TILE = 256
GATE_PAD = 128
GATE_ROWS = 4 * HEADS
ROW_SLAB = 16
CTX_ROWS = 8
NORM_COLS = 128

VMEM_LIMIT = 56 * 1024 * 1024


def _sigmoid(x):
    return 1.0 / (1.0 + jnp.exp(-x))


def _log_sigmoid(x):
    return jnp.minimum(x, 0.0) - jnp.log(1.0 + jnp.exp(-jnp.abs(x)))


def _gelu_tanh(x):
    c2 = -2.0 * 0.7978845608028654
    return x / (1.0 + jnp.exp(x * (c2 + (c2 * 0.044715) * (x * x))))


def _rms(x, g):
    return x * lax.rsqrt(jnp.mean(x * x, axis=-1, keepdims=True) + EPS) * g


def _norm_mod(x, g, shift, scale):
    return _rms(x, g) * (1.0 + scale) + shift


def _dot(a, b):
    return jnp.dot(a, b, preferred_element_type=F32)


def _ada_kernel(c_ref, w_ref, b_ref, o_ref):
    c = c_ref[...]
    s = (c * _sigmoid(c)).astype(BF16)
    o_ref[...] = _dot(s, w_ref[...].astype(BF16)) + b_ref[...]


def _adaln(cond, w, b, tn=1024):
    rows, d = cond.shape
    n = w.shape[1]
    return pl.pallas_call(
        _ada_kernel,
        grid=(n // tn,),
        in_specs=[pl.BlockSpec((rows, d), lambda j: (0, 0)),
                  pl.BlockSpec((d, tn), lambda j: (0, j)),
                  pl.BlockSpec((1, tn), lambda j: (0, j))],
        out_specs=pl.BlockSpec((rows, tn), lambda j: (0, j)),
        out_shape=jax.ShapeDtypeStruct((rows, n), F32),
        compiler_params=pltpu.CompilerParams(dimension_semantics=("parallel",),
                                             vmem_limit_bytes=VMEM_LIMIT),
        name="adaln",
    )(cond, w, b.reshape(1, n))


def _ffn_kernel(*refs, mod_base, final):
    if final:
        x_ref, mod_ref, g_ref, wg_ref, wu_ref, wo_ref, fn_ref, o_ref, xn_ref, r_ref = refs
    else:
        x_ref, mod_ref, g_ref, wg_ref, wu_ref, wo_ref, o_ref, xn_ref = refs
    f = pl.program_id(2)
    tm = xn_ref.shape[0]

    @pl.when(f == 0)
    def _():
        shift = mod_ref[0, mod_base:mod_base + 1, :]
        gain = g_ref[...] * (1.0 + mod_ref[0, mod_base + 1:mod_base + 2, :])

        def slab(i, carry):
            rows = pl.ds(pl.multiple_of(i * ROW_SLAB, ROW_SLAB), ROW_SLAB)
            x = x_ref[0, rows, :]
            r = lax.rsqrt(jnp.mean(x * x, axis=-1, keepdims=True) + EPS)
            xn_ref[rows, :] = (x * r * gain + shift).astype(BF16)
            o_ref[0, rows, :] = jnp.zeros_like(x)
            return carry

        lax.fori_loop(0, tm // ROW_SLAB, slab, 0, unroll=8)

    xn = xn_ref[...]
    gate = _dot(xn, wg_ref[...])
    up = _dot(xn, wu_ref[...])
    h = (gate * _sigmoid(gate) * up).astype(BF16)
    o_ref[0] += _dot(h, wo_ref[...])

    @pl.when(f == pl.num_programs(2) - 1)
    def _():
        half_gate = 0.5 * mod_ref[0, mod_base + 2:mod_base + 3, :]

        def slab(i, carry):
            rows = pl.ds(pl.multiple_of(i * ROW_SLAB, ROW_SLAB), ROW_SLAB)
            y = x_ref[0, rows, :] + half_gate * o_ref[0, rows, :]
            o_ref[0, rows, :] = y
            if final:
                r_ref[rows, :] = lax.rsqrt(jnp.mean(y * y, axis=-1, keepdims=True) + EPS)
            return carry

        lax.fori_loop(0, tm // ROW_SLAB, slab, 0, unroll=8)

        if final:
            def scale(i, carry):
                rows = pl.ds(pl.multiple_of(i * ROW_SLAB, ROW_SLAB), ROW_SLAB)
                o_ref[0, rows, :] = o_ref[0, rows, :] * r_ref[rows, :] * fn_ref[...]
                return carry

            lax.fori_loop(0, tm // ROW_SLAB, scale, 0, unroll=8)


def _ffn(x, mods, mod_row, mod_base, g, w_in, w_out, final_g=None, tm=1024, tf=512):
    bsz, s, d = x.shape
    ff = w_out.shape[0]
    tm = min(tm, s)
    nf = ff // tf
    final = final_g is not None
    in_specs = [pl.BlockSpec((1, tm, d), lambda b, i, f: (b, i, 0)),
                pl.BlockSpec((1, N_MOD, d), lambda b, i, f: (mod_row(b), 0, 0)),
                pl.BlockSpec((1, d), lambda b, i, f: (0, 0)),
                pl.BlockSpec((d, tf), lambda b, i, f: (0, f)),
                pl.BlockSpec((d, tf), lambda b, i, f: (0, nf + f)),
                pl.BlockSpec((tf, d), lambda b, i, f: (f, 0))]
    args = [x, mods, g.reshape(1, d), w_in, w_in, w_out]
    if final:
        in_specs.append(pl.BlockSpec((1, d), lambda b, i, f: (0, 0)))
        args.append(final_g.reshape(1, d))
    return pl.pallas_call(
        functools.partial(_ffn_kernel, mod_base=mod_base, final=final),
        grid=(bsz, s // tm, nf),
        in_specs=in_specs,
        out_specs=pl.BlockSpec((1, tm, d), lambda b, i, f: (b, i, 0)),
        out_shape=jax.ShapeDtypeStruct((bsz, s, d), F32),
        scratch_shapes=[pltpu.VMEM((tm, d), BF16)] + ([pltpu.VMEM((tm, 1), F32)] if final else []),
        compiler_params=pltpu.CompilerParams(
            dimension_semantics=("parallel", "parallel", "arbitrary"),
            vmem_limit_bytes=VMEM_LIMIT),
        name="ffn_final" if final else "ffn",
    )(*args)


def _inproj_kernel(*refs, ctx_mode, tm):
    if ctx_mode:
        (x_ref, xp_ref, xq_ref, mod_ref, g_ref, wqk_ref, wv_ref, wg_ref, cw_ref, cb_ref,
         k_ref, v_ref, zg_ref, zs_ref, xn_ref, zgs_ref) = refs
    else:
        (x_ref, xp_ref, xq_ref, mod_ref, g_ref, wqk_ref, wv_ref, wg_ref, cw_ref, cb_ref,
         wo_ref, wm_ref, gn_ref,
         q_ref, k_ref, v_ref, zg_ref, o_ref, u_ref, vn_ref, zs_ref, xn_ref, zgs_ref, gv_ref) = refs
    i = pl.program_id(1)
    g = g_ref[...]
    shift = mod_ref[0, 3:4, :]
    gain = g * (1.0 + mod_ref[0, 4:5, :])

    def norm_rows(x):
        r = lax.rsqrt(jnp.mean(x * x, axis=-1, keepdims=True) + EPS)
        return (x * r * gain + shift).astype(BF16)

    xn_ref[0:HALO, :] = norm_rows(xp_ref[0])
    xn_ref[HALO + tm:, :] = norm_rows(xq_ref[0])

    def slab(t, carry):
        rows = pl.ds(pl.multiple_of(t * ROW_SLAB, ROW_SLAB), ROW_SLAB)
        xn_ref[pl.ds(pl.multiple_of(HALO + t * ROW_SLAB, ROW_SLAB), ROW_SLAB), :] = norm_rows(x_ref[0, rows, :])
        return carry

    lax.fori_loop(0, tm // ROW_SLAB, slab, 0, unroll=8)
    xn = xn_ref[HALO:HALO + tm, :]

    def col(t):
        return slice(t * COL_TILE, (t + 1) * COL_TILE)

    if not ctx_mode:
        gw = wm_ref.shape[1] // 2
        ssq = jnp.zeros((tm, 1), F32)
        for t in range(gw // COL_TILE):
            gv = _gelu_tanh(_dot(xn, wm_ref[:, gw + t * COL_TILE:gw + (t + 1) * COL_TILE]))
            gv_ref[:, col(t)] = gv
            ssq = ssq + jnp.sum(gv * gv, axis=-1, keepdims=True)
        for t in range(gw // COL_TILE):
            u_ref[0, :, col(t)] = _gelu_tanh(_dot(xn, wm_ref[:, col(t)])).astype(BF16)

    wq = wqk_ref.shape[1]
    k_off = 0 if ctx_mode else wq // 2
    for t in range(wq // COL_TILE):
        zs_ref[...] = _dot(xn_ref[...], wqk_ref[:, col(t)])
        zs_ref[0:HALO, :] = jnp.where(i == 0, 0.0, zs_ref[0:HALO, :])
        zs_ref[HALO + tm:, :] = jnp.where(i == pl.num_programs(1) - 1, 0.0, zs_ref[HALO + tm:, :])
        conv = cb_ref[:, col(t)]
        for j in range(CONV_K):
            off = HALO - CONV_K // 2 + j
            conv = conv + zs_ref[off:off + tm, :] * cw_ref[j:j + 1, col(t)]
        qk = conv * _sigmoid(conv)
        for hh in range(COL_TILE // DK):
            c0 = t * COL_TILE + hh * DK
            if c0 < k_off:
                q_ref[0, c0 // DK] = (qk[:, hh * DK:(hh + 1) * DK] * (DK ** -0.5)).astype(BF16)
            else:
                for cc in range(tm // CHUNK):
                    k_ref[0, (c0 - k_off) // DK, cc] = (
                        qk[cc * CHUNK:(cc + 1) * CHUNK, hh * DK:(hh + 1) * DK].T.astype(BF16))

    if not ctx_mode:
        for h in range(HEADS):
            o_ref[0, h] = _sigmoid(_dot(xn, wo_ref[:, h * DV:(h + 1) * DV])).astype(BF16)
        r = lax.rsqrt(ssq * (1.0 / gw) + EPS)
        vn_ref[0] = (gv_ref[...] * r * gn_ref[...]).astype(BF16)

    for h in range(HEADS):
        v_ref[0, h] = _dot(xn, wv_ref[:, h * DV:(h + 1) * DV]).astype(BF16)
    zgs_ref[...] = _dot(xn, wg_ref[...])
    zg_ref[0] = zgs_ref[...].T[:GATE_ROWS, :]


def _const_spec(shape):
    nd = len(shape)
    return pl.BlockSpec(shape, lambda b, i: (0,) * nd, pipeline_mode=pl.Buffered(1))


def _inproj(x, mods, mod_row, g, wqk, wv, wg, cw, cb, wo=None, wm=None, gn=None, tm=512):
    bsz, s, d = x.shape
    ctx_mode = wo is None
    tm = min(tm, s)
    nh = s // HALO
    r = tm // HALO
    wq = wqk.shape[1]
    in_specs = [pl.BlockSpec((1, tm, d), lambda b, i: (b, i, 0)),
                pl.BlockSpec((1, HALO, d), lambda b, i: (b, jnp.maximum(i * r - 1, 0), 0)),
                pl.BlockSpec((1, HALO, d), lambda b, i: (b, jnp.minimum((i + 1) * r, nh - 1), 0)),
                pl.BlockSpec((1, N_MOD, d), lambda b, i: (mod_row(b), 0, 0)),
                _const_spec((1, d)), _const_spec(wqk.shape), _const_spec(wv.shape),
                _const_spec(wg.shape), _const_spec(cw.shape), _const_spec(cb.shape)]
    args = [x, x, x, mods, g.reshape(1, d), wqk, wv, wg, cw, cb]

    def tok(width, dtype):
        return (pl.BlockSpec((1, tm, width), lambda b, i: (b, i, 0)),
                jax.ShapeDtypeStruct((bsz, s, width), dtype))

    kt = (pl.BlockSpec((1, HEADS, tm // CHUNK, DK, CHUNK), lambda b, i: (b, 0, i, 0, 0)),
          jax.ShapeDtypeStruct((bsz, HEADS, s // CHUNK, DK, CHUNK), BF16))
    zgt = (pl.BlockSpec((1, GATE_ROWS, tm), lambda b, i: (b, 0, i)),
           jax.ShapeDtypeStruct((bsz, GATE_ROWS, s), F32))
    def heads(width):
        return (pl.BlockSpec((1, HEADS, tm, width), lambda b, i: (b, 0, i, 0)),
                jax.ShapeDtypeStruct((bsz, HEADS, s, width), BF16))

    if ctx_mode:
        outs = [kt, heads(DV), zgt]
    else:
        in_specs += [_const_spec(wo.shape), _const_spec(wm.shape), _const_spec(gn.shape)]
        args += [wo, wm, gn]
        outs = [heads(DK), kt, heads(DV), zgt,
                heads(DV), tok(wm.shape[1] // 2, BF16), tok(wm.shape[1] // 2, BF16)]
    return pl.pallas_call(
        functools.partial(_inproj_kernel, ctx_mode=ctx_mode, tm=tm),
        grid=(bsz, s // tm),
        in_specs=in_specs,
        out_specs=[o[0] for o in outs],
        out_shape=[o[1] for o in outs],
        scratch_shapes=[pltpu.VMEM((tm + 2 * HALO, COL_TILE), F32), pltpu.VMEM((tm + 2 * HALO, d), BF16),
                        pltpu.VMEM((tm, GATE_PAD), F32)]
        + ([] if ctx_mode else [pltpu.VMEM((tm, wm.shape[1] // 2), F32)]),
        compiler_params=pltpu.CompilerParams(dimension_semantics=("parallel", "parallel"),
                                             vmem_limit_bytes=VMEM_LIMIT),
        name="inproj_ctx" if ctx_mode else "inproj",
    )(*args)


def _lane_scan(x, op, ident, reverse):
    lane = lax.broadcasted_iota(jnp.int32, x.shape, 1)
    n = x.shape[1]
    k = 1
    while k < n:
        if reverse:
            x = op(x, jnp.where(lane < n - k, pltpu.roll(x, n - k, axis=1), ident))
        else:
            x = op(x, jnp.where(lane >= k, pltpu.roll(x, k, axis=1), ident))
        k *= 2
    return x


def _mlstm_kernel(q_ref, kt_ref, v_ref, o_ref, g_ref, ktc_ref, vc_ref, br_ref, ng_ref, out_ref,
                  e_ref, col_ref, tot_ref, emax_ref, rhs_ref, ms_ref, c_ref, m_ref, lhs_ref, floor_ref):
    s = q_ref.shape[2]
    nc = s // CHUNK
    ncc = vc_ref.shape[2] // CHUNK
    row = lax.broadcasted_iota(jnp.int32, (CHUNK, CHUNK), 0)
    col = lax.broadcasted_iota(jnp.int32, (CHUNK, CHUNK), 1)
    eye = row == col
    ones = jnp.ones((CHUNK, NORM_COLS), BF16)
    sel_r = lax.broadcasted_iota(jnp.int32, (3 * CHUNK, 2 * NORM_COLS), 0)
    sel_c = lax.broadcasted_iota(jnp.int32, (3 * CHUNK, 2 * NORM_COLS), 1)
    sel = jnp.where((sel_r < CHUNK) == (sel_c < NORM_COLS), 1.0, 0.0).astype(BF16)

    for d in range(2):
        logi = g_ref[0, 2 * d, 0] + br_ref[0, 2 * d]
        logf = _log_sigmoid(g_ref[0, 2 * d + 1, 0] + br_ref[0, 2 * d + 1])
        b = _lane_scan(logf, jnp.add, 0.0, d == 1)
        e = logi - b
        e_ref[d] = e
        b_hi = b.astype(BF16).astype(F32)
        col_ref[d, 0] = _lane_scan(e, jnp.maximum, -jnp.inf, d == 1).astype(BF16).astype(F32)
        col_ref[d, 1] = b_hi
        col_ref[d, 2] = (b - b_hi).astype(BF16).astype(F32)
        tot_ref[d] = jnp.sum(logf, axis=-1, keepdims=True)
        emax_ref[d] = jnp.max(e, axis=-1, keepdims=True)

    def scan_step(d, r, kt, v):
        e_r = e_ref[d, pl.ds(r, 1), :]
        emax = emax_ref[d, pl.ds(r, 1), :]
        kw = (kt.astype(F32) * jnp.exp(e_r - emax)).astype(BF16)
        v_aug = jnp.concatenate([v, ones], axis=1)
        upd = _dot(kw, v_aug)
        m = m_ref[d]
        c = c_ref[d]
        mx = jnp.maximum(m, emax)
        c_ref[d] = jnp.exp(m - mx) * c + jnp.exp(emax - mx) * upd
        m_ref[d] = tot_ref[d, pl.ds(r, 1), :] + mx
        return c.astype(BF16), m, v_aug

    c_ref[...] = jnp.zeros_like(c_ref)
    m_ref[...] = jnp.zeros_like(m_ref)
    for t in range(ncc):
        for d in range(2):
            ci = t if d == 0 else ncc - 1 - t
            scan_step(d, ci, ktc_ref[0, 0, ci], vc_ref[0, 0, ci * CHUNK:(ci + 1) * CHUNK, :])

    def scan_body(t, carry):
        for d in range(2):
            ci = t if d == 0 else nc - 1 - t
            rows = pl.ds(pl.multiple_of(ci * CHUNK, CHUNK), CHUNK)
            c_in, m_in, v_aug = scan_step(d, CTX_ROWS + ci, kt_ref[0, 0, ci], v_ref[0, 0, rows, :])
            rhs_ref[d, ci, 0:DK, :] = c_in
            rhs_ref[d, ci, DK:, :] = v_aug
            ms_ref[d, ci] = m_in
        return carry

    lax.fori_loop(0, nc, scan_body, 0, unroll=4)

    def weights(ci, slot):
        q = q_ref[0, 0, pl.ds(pl.multiple_of(ci * CHUNK, CHUNK), CHUNK), :]
        qf = q.astype(F32)
        qk = _dot(q, kt_ref[0, 0, ci])
        r = pl.ds(CTX_ROWS + ci, 1)
        for d in range(2):
            inc = (col <= row) if d == 0 else (col >= row)
            e_m = jnp.where(inc, e_ref[d, r, :], -jnp.inf)
            diag = jnp.concatenate([jnp.where(eye, col_ref[d, i, r, :], 0.0) for i in range(3)], axis=1)
            rep = _dot(diag.astype(BF16), sel)
            m = ms_ref[d, ci]
            mm = jnp.maximum(m, rep[:, :NORM_COLS])
            p = (qk * jnp.exp(e_m - mm)).astype(BF16)
            qa = (qf * jnp.exp(m - mm)).astype(BF16)
            lhs_ref[slot, d] = jnp.concatenate([qa, p], axis=1)
            floor_ref[slot, d] = jnp.exp(-(rep[:, NORM_COLS:] + mm))

    def apply(ci, slot):
        rows = pl.ds(pl.multiple_of(ci * CHUNK, CHUNK), CHUNK)
        h = None
        for d in range(2):
            comb = _dot(lhs_ref[slot, d], rhs_ref[d, ci])
            inv = 1.0 / jnp.maximum(jnp.abs(comb[:, DV:]), floor_ref[slot, d])
            hd = comb[:, :DV] * jnp.concatenate([inv] * (DV // NORM_COLS), axis=1)
            h = hd if h is None else h + hd
        out_ref[0, 0, rows, :] = (_rms(h, ng_ref[...]) * o_ref[0, 0, rows, :].astype(F32)).astype(BF16)

    def out_body(ci, carry):
        apply(ci - 1, (ci - 1) % 2)
        weights(ci, ci % 2)
        return carry

    weights(0, 0)
    lax.fori_loop(1, nc, out_body, 0, unroll=4)
    apply(nc - 1, (nc - 1) % 2)


def _mlstm(q, kt, v, o, g, ktc, vc, bias, norm_g):
    bsz, _, s, _ = q.shape
    sc = vc.shape[2]
    nc, ncc = s // CHUNK, sc // CHUNK
    nr = g.shape[3]
    return pl.pallas_call(
        _mlstm_kernel,
        grid=(bsz, HEADS),
        in_specs=[pl.BlockSpec((1, 1, s, DK), lambda b, h: (b, h, 0, 0)),
                  pl.BlockSpec((1, 1, nc, DK, CHUNK), lambda b, h: (b, h, 0, 0, 0)),
                  pl.BlockSpec((1, 1, s, DV), lambda b, h: (b, h, 0, 0)),
                  pl.BlockSpec((1, 1, s, DV), lambda b, h: (b, h, 0, 0)),
                  pl.BlockSpec((1, 4, 1, nr, CHUNK), lambda b, h: (b, 0, h, 0, 0)),
                  pl.BlockSpec((1, 1, ncc, DK, CHUNK), lambda b, h: (b, h, 0, 0, 0)),
                  pl.BlockSpec((1, 1, sc, DV), lambda b, h: (b, h, 0, 0)),
                  pl.BlockSpec((1, 4, 1, 1), lambda b, h: (h, 0, 0, 0)),
                  pl.BlockSpec((1, DV), lambda b, h: (0, h))],
        out_specs=pl.BlockSpec((1, 1, s, DV), lambda b, h: (b, h, 0, 0)),
        out_shape=jax.ShapeDtypeStruct((bsz, HEADS, s, DV), BF16),
        scratch_shapes=[pltpu.VMEM((2, nr, CHUNK), F32), pltpu.VMEM((2, 3, nr, CHUNK), F32),
                        pltpu.VMEM((2, nr, 1), F32), pltpu.VMEM((2, nr, 1), F32),
                        pltpu.VMEM((2, nc, DK + CHUNK, DV + NORM_COLS), BF16), pltpu.VMEM((2, nc, 1, 1), F32),
                        pltpu.VMEM((2, DK, DV + NORM_COLS), F32), pltpu.VMEM((2, 1, 1), F32),
                        pltpu.VMEM((2, 2, CHUNK, DK + CHUNK), BF16), pltpu.VMEM((2, 2, CHUNK, NORM_COLS), F32)],
        compiler_params=pltpu.CompilerParams(dimension_semantics=("parallel", "parallel"),
                                             vmem_limit_bytes=VMEM_LIMIT),
        name="mlstm",
    )(q, kt, v, o, g, ktc, vc, bias, norm_g)


def _outproj_kernel(x_ref, mod_ref, hx_ref, u_ref, vn_ref, gw_ref, gb_ref, wt_ref, wb_ref, o_ref, gx_ref):
    tm = x_ref.shape[1]
    for ci in range(tm // CHUNK):
        rows = slice(ci * CHUNK, (ci + 1) * CHUNK)
        for g in range(GROUPS):
            cols = slice(g * GD, (g + 1) * GD)
            sp = _dot(gw_ref[g], vn_ref[0, rows, cols]) + gb_ref[:, cols]
            gx_ref[rows, cols] = (u_ref[0, rows, cols].astype(F32) * sp).astype(BF16)
    y = _dot(gx_ref[...], wb_ref[...])
    for h in range(HEADS):
        y = y + _dot(hx_ref[0, h], wt_ref[h * DV:(h + 1) * DV, :])
    o_ref[0] = x_ref[0] + mod_ref[0, 5:6, :] * y


def _outproj(x, mods, hx, u, vn, gw, gb, wt, wb, tm=512):
    bsz, s, d = x.shape
    w = u.shape[2]
    return pl.pallas_call(
        _outproj_kernel,
        grid=(bsz, s // tm),
        in_specs=[pl.BlockSpec((1, tm, d), lambda b, i: (b, i, 0)),
                  pl.BlockSpec((1, N_MOD, d), lambda b, i: (b, 0, 0)),
                  pl.BlockSpec((1, HEADS, tm, DV), lambda b, i: (b, 0, i, 0)),
                  pl.BlockSpec((1, tm, w), lambda b, i: (b, i, 0)),
                  pl.BlockSpec((1, tm, w), lambda b, i: (b, i, 0)),
                  _const_spec(gw.shape), _const_spec(gb.shape),
                  _const_spec(wt.shape), _const_spec(wb.shape)],
        out_specs=pl.BlockSpec((1, tm, d), lambda b, i: (b, i, 0)),
        out_shape=jax.ShapeDtypeStruct((bsz, s, d), F32),
        scratch_shapes=[pltpu.VMEM((tm, w), BF16)],
        compiler_params=pltpu.CompilerParams(dimension_semantics=("parallel", "parallel"),
                                             vmem_limit_bytes=VMEM_LIMIT),
        name="outproj",
    )(x, mods, hx, u, vn, gw, gb, wt, wb)


def _gate_rows(zgc, zg):
    def rows(z):
        bsz, _, t = z.shape
        return z.reshape(bsz, 4, HEADS, t // CHUNK, CHUNK)
    gc = rows(zgc)
    assert gc.shape[3] <= CTX_ROWS
    gc = jnp.pad(gc, ((0, 0), (0, 0), (0, 0), (0, CTX_ROWS - gc.shape[3]), (0, 0)))
    return jnp.concatenate([gc, rows(zg)], axis=3)


def kernel(x, c, ctx, c_ctx, w_ada, b_ada, norm_ffn1, w_ffn1_in, w_ffn1_out, norm_mix, w_in, conv_w, conv_b, b_igate, b_fgate, mlstm_norm, gmlp_norm, gmlp_w, gmlp_b, w_out, norm_ffn2, w_ffn2_in, w_ffn2_out, final_norm):
    bsz, s, d = x.shape
    depth = w_ada.shape[0]
    assert depth == 1, "context-stream update for non-final layers is not implemented"
    l = 0
    qk_w = 2 * HEADS * DK
    mw = HEADS * DV
    c_qk, c_v, c_o = qk_w, qk_w + mw, qk_w + 2 * mw
    c_gate = c_o + 4 * HEADS

    lat_row = lambda b: b
    ctx_row = lambda b: bsz

    rows = -(-(bsz + 1) // 8) * 8
    cond = jnp.zeros((rows, d), F32).at[:bsz].set(c).at[bsz].set(c_ctx)
    mods = _adaln(cond, w_ada[l], b_ada[l]).reshape(rows, N_MOD, d)

    w1i, w1o = w_ffn1_in[l].astype(BF16), w_ffn1_out[l].astype(BF16)
    w2i, w2o = w_ffn2_in[l].astype(BF16), w_ffn2_out[l].astype(BF16)
    wi = w_in[l]
    wqk, wv, wo = (wi[:, a:b].astype(BF16) for a, b in ((0, c_qk), (c_qk, c_v), (c_v, c_o)))
    wg = jnp.pad(wi[:, c_o:c_gate].astype(BF16), ((0, 0), (0, GATE_PAD - 4 * HEADS)))
    wm = wi[:, c_gate:].astype(BF16)
    cw = jnp.pad(conv_w[l], ((0, 8 - CONV_K), (0, 0)))
    cb = conv_b[l].reshape(1, qk_w)
    wout = w_out[l].astype(BF16)

    x = _ffn(x, mods, lat_row, 0, norm_ffn1[l], w1i, w1o)
    ctx = _ffn(ctx.reshape(1, -1, d), mods, ctx_row, 0, norm_ffn1[l], w1i, w1o).reshape(ctx.shape)

    q, kt, v, zg, o, u, vn = _inproj(x, mods, lat_row, norm_mix[l], wqk, wv, wg, cw, cb,
                                     wo, wm, gmlp_norm[l].reshape(1, -1))
    ktc, vc, zgc = _inproj(ctx, mods, ctx_row, norm_mix[l], wqk[:, qk_w // 2:], wv, wg,
                           cw[:, qk_w // 2:], cb[:, qk_w // 2:])

    bias = jnp.stack([b_igate[l][0], b_fgate[l][0], b_igate[l][1], b_fgate[l][1]], axis=-1)
    hx = _mlstm(q, kt, v, o, _gate_rows(zgc, zg), ktc, vc, bias.reshape(HEADS, 4, 1, 1),
                mlstm_norm[l].reshape(1, mw))

    gb = jnp.repeat(gmlp_b[l].T, GD, axis=1)
    x = _outproj(x, mods, hx, u, vn, gmlp_w[l].astype(BF16), gb, wout[:mw], wout[mw:])

    return _ffn(x, mods, lat_row, 6, norm_ffn2[l], w2i, w2o, final_g=final_norm)
```

```python
import functools

import jax
import jax.numpy as jnp
from jax import lax
from jax.experimental import pallas as pl
from jax.experimental.pallas import tpu as pltpu

F32 = jnp.float32
BF16 = jnp.bfloat16

EPS = 1e-6
N_MOD = 9
HEADS = 4
DK = 128
DV = 256
CHUNK = 128
GROUPS = 8
GD = 128
CONV_K = 5
HALO = 16
COL_TILE = 256
GATE_PAD = 128
GATE_ROWS = 4 * HEADS
ROW_SLAB = 16
CTX_ROWS = 8
NORM_COLS = 128

VMEM_LIMIT = 56 * 1024 * 1024


def _sigmoid(x):
    return 1.0 / (1.0 + jnp.exp(-x))


def _log_sigmoid(x):
    return jnp.minimum(x, 0.0) - jnp.log(1.0 + jnp.exp(-jnp.abs(x)))


def _gelu_tanh(x):
    c2 = -2.0 * 0.7978845608028654
    return x / (1.0 + jnp.exp(x * (c2 + (c2 * 0.044715) * (x * x))))


def _rms(x, g):
    return x * lax.rsqrt(jnp.mean(x * x, axis=-1, keepdims=True) + EPS) * g


def _norm_mod(x, g, shift, scale):
    return _rms(x, g) * (1.0 + scale) + shift


def _dot(a, b):
    return jnp.dot(a, b, preferred_element_type=F32)


def _ada_kernel(c_ref, w_ref, b_ref, o_ref):
    c = c_ref[...]
    s = (c * _sigmoid(c)).astype(BF16)
    o_ref[...] = _dot(s, w_ref[...].astype(BF16)) + b_ref[...]


def _adaln(cond, w, b, tn=1024):
    rows, d = cond.shape
    n = w.shape[1]
    return pl.pallas_call(
        _ada_kernel,
        grid=(n // tn,),
        in_specs=[pl.BlockSpec((rows, d), lambda j: (0, 0)),
                  pl.BlockSpec((d, tn), lambda j: (0, j)),
                  pl.BlockSpec((1, tn), lambda j: (0, j))],
        out_specs=pl.BlockSpec((rows, tn), lambda j: (0, j)),
        out_shape=jax.ShapeDtypeStruct((rows, n), F32),
        compiler_params=pltpu.CompilerParams(dimension_semantics=("parallel",),
                                             vmem_limit_bytes=VMEM_LIMIT),
        name="adaln",
    )(cond, w, b.reshape(1, n))


def _ffn_kernel(*refs, mod_base, final, nf):
    if final:
        x_ref, mod_ref, modn_ref, g_ref, wg_ref, wu_ref, wo_ref, fn_ref, o_ref, xn_ref, h_ref, r_ref = refs
    else:
        x_ref, mod_ref, modn_ref, g_ref, wg_ref, wu_ref, wo_ref, o_ref, xn_ref, h_ref = refs
    t = pl.program_id(0)
    f = pl.program_id(1)
    tm = xn_ref.shape[0]

    def norm(m_ref):
        shift = m_ref[0, mod_base:mod_base + 1, :]
        gain = g_ref[...] * (1.0 + m_ref[0, mod_base + 1:mod_base + 2, :])
        for i in range(tm // ROW_SLAB):
            rows = slice(i * ROW_SLAB, (i + 1) * ROW_SLAB)
            x = x_ref[rows, :]
            r = lax.rsqrt(jnp.mean(x * x, axis=-1, keepdims=True) + EPS)
            xn_ref[rows, :] = (x * r * gain + shift).astype(BF16)

    def stage_a():
        xn = xn_ref[...]
        gate = _dot(xn, wg_ref[...])
        up = _dot(xn, wu_ref[...])
        h_ref[...] = (gate * _sigmoid(gate) * up).astype(BF16)

    def stage_b():
        o_ref[...] += (0.5 * mod_ref[0, mod_base + 2:mod_base + 3, :]) * _dot(h_ref[...], wo_ref[...])

    @pl.when((f == 0) & (t == 0))
    def _():
        norm(mod_ref)

    @pl.when(f == 0)
    def _():
        o_ref[...] = x_ref[...]
        stage_a()

    @pl.when((f > 0) & (f < nf))
    def _():
        stage_b()
        stage_a()

    @pl.when(f == nf)
    def _():
        norm(modn_ref)
        stage_b()

        if final:
            def stats(i, carry):
                rows = pl.ds(pl.multiple_of(i * ROW_SLAB, ROW_SLAB), ROW_SLAB)
                y = o_ref[rows, :]
                r_ref[rows, :] = lax.rsqrt(jnp.mean(y * y, axis=-1, keepdims=True) + EPS)
                return carry

            lax.fori_loop(0, tm // ROW_SLAB, stats, 0, unroll=8)

            def scale(i, carry):
                rows = pl.ds(pl.multiple_of(i * ROW_SLAB, ROW_SLAB), ROW_SLAB)
                o_ref[rows, :] = o_ref[rows, :] * r_ref[rows, :] * fn_ref[...]
                return carry

            lax.fori_loop(0, tm // ROW_SLAB, scale, 0, unroll=8)


def _ffn(x, mods, tile_row, mod_base, g, w_in, w_out, final_g=None, tm=1024, tf=512):
    n, d = x.shape
    ff = w_out.shape[0]
    nf = ff // tf
    tm = min(tm, n)
    nt = n // tm
    final = final_g is not None

    def x_map(t, f):
        return (jnp.minimum(t + (f == nf).astype(jnp.int32), nt - 1), 0)

    def up_map(half):
        return lambda t, f: (0, half * nf + jnp.minimum(f, nf - 1))

    in_specs = [pl.BlockSpec((tm, d), x_map),
                pl.BlockSpec((1, N_MOD, d), lambda t, f: (tile_row(t), 0, 0)),
                pl.BlockSpec((1, N_MOD, d), lambda t, f: (tile_row(jnp.minimum(t + 1, nt - 1)), 0, 0)),
                pl.BlockSpec((1, d), lambda t, f: (0, 0)),
                pl.BlockSpec((d, tf), up_map(0)),
                pl.BlockSpec((d, tf), up_map(1)),
                pl.BlockSpec((tf, d), lambda t, f: (jnp.maximum(f - 1, 0), 0))]
    args = [x, mods, mods, g.reshape(1, d), w_in, w_in, w_out]
    if final:
        in_specs.append(pl.BlockSpec((1, d), lambda t, f: (0, 0)))
        args.append(final_g.reshape(1, d))
    return pl.pallas_call(
        functools.partial(_ffn_kernel, mod_base=mod_base, final=final, nf=nf),
        grid=(nt, nf + 1),
        in_specs=in_specs,
        out_specs=pl.BlockSpec((tm, d), lambda t, f: (t, 0)),
        out_shape=jax.ShapeDtypeStruct((n, d), F32),
        scratch_shapes=[pltpu.VMEM((tm, d), BF16), pltpu.VMEM((tm, tf), BF16)]
        + ([pltpu.VMEM((tm, 1), F32)] if final else []),
        compiler_params=pltpu.CompilerParams(
            dimension_semantics=("arbitrary", "arbitrary"),
            vmem_limit_bytes=VMEM_LIMIT),
        name="ffn_final" if final else "ffn",
    )(*args)


def _inproj_kernel(*refs, ctx_mode, tm):
    if ctx_mode:
        (x_ref, xp_ref, xq_ref, mod_ref, g_ref, wqk_ref, wv_ref, wg_ref, cw_ref, cb_ref,
         k_ref, v_ref, zg_ref, zs_ref, xn_ref, zgs_ref) = refs
    else:
        (x_ref, xp_ref, xq_ref, mod_ref, g_ref, wqk_ref, wv_ref, wg_ref, cw_ref, cb_ref,
         wo_ref, wm_ref, gn_ref,
         q_ref, k_ref, v_ref, zg_ref, o_ref, u_ref, vn_ref, zs_ref, xn_ref, zgs_ref, gv_ref) = refs
    i = pl.program_id(1)
    g = g_ref[...]
    shift = mod_ref[0, 3:4, :]
    gain = g * (1.0 + mod_ref[0, 4:5, :])

    def norm_rows(x):
        r = lax.rsqrt(jnp.mean(x * x, axis=-1, keepdims=True) + EPS)
        return (x * r * gain + shift).astype(BF16)

    xn_ref[0:HALO, :] = norm_rows(xp_ref[0])
    xn_ref[HALO + tm:, :] = norm_rows(xq_ref[0])

    def slab(t, carry):
        rows = pl.ds(pl.multiple_of(t * ROW_SLAB, ROW_SLAB), ROW_SLAB)
        xn_ref[pl.ds(pl.multiple_of(HALO + t * ROW_SLAB, ROW_SLAB), ROW_SLAB), :] = norm_rows(x_ref[0, rows, :])
        return carry

    lax.fori_loop(0, tm // ROW_SLAB, slab, 0, unroll=8)
    xn = xn_ref[HALO:HALO + tm, :]

    def col(t):
        return slice(t * COL_TILE, (t + 1) * COL_TILE)

    if not ctx_mode:
        gw = wm_ref.shape[1] // 2
        ssq = jnp.zeros((tm, 1), F32)
        for t in range(gw // COL_TILE):
            gv = _gelu_tanh(_dot(xn, wm_ref[:, gw + t * COL_TILE:gw + (t + 1) * COL_TILE]))
            gv_ref[:, col(t)] = gv
            ssq = ssq + jnp.sum(gv * gv, axis=-1, keepdims=True)
        for t in range(gw // COL_TILE):
            u_ref[0, :, col(t)] = _gelu_tanh(_dot(xn, wm_ref[:, col(t)])).astype(BF16)

    wq = wqk_ref.shape[1]
    k_off = 0 if ctx_mode else wq // 2
    for t in range(wq // COL_TILE):
        zs_ref[...] = _dot(xn_ref[...], wqk_ref[:, col(t)])
        zs_ref[0:HALO, :] = jnp.where(i == 0, 0.0, zs_ref[0:HALO, :])
        zs_ref[HALO + tm:, :] = jnp.where(i == pl.num_programs(1) - 1, 0.0, zs_ref[HALO + tm:, :])
        conv = cb_ref[:, col(t)]
        for j in range(CONV_K):
            off = HALO - CONV_K // 2 + j
            conv = conv + zs_ref[off:off + tm, :] * cw_ref[j:j + 1, col(t)]
        qk = conv * _sigmoid(conv)
        for hh in range(COL_TILE // DK):
            c0 = t * COL_TILE + hh * DK
            if c0 < k_off:
                q_ref[0, c0 // DK] = (qk[:, hh * DK:(hh + 1) * DK] * (DK ** -0.5)).astype(BF16)
            else:
                for cc in range(tm // CHUNK):
                    k_ref[0, (c0 - k_off) // DK, cc] = (
                        qk[cc * CHUNK:(cc + 1) * CHUNK, hh * DK:(hh + 1) * DK].T.astype(BF16))

    if not ctx_mode:
        for h in range(HEADS):
            o_ref[0, h] = _sigmoid(_dot(xn, wo_ref[:, h * DV:(h + 1) * DV])).astype(BF16)
        r = lax.rsqrt(ssq * (1.0 / gw) + EPS)
        vn_ref[0] = (gv_ref[...] * r * gn_ref[...]).astype(BF16)

    for h in range(HEADS):
        v_ref[0, h] = _dot(xn, wv_ref[:, h * DV:(h + 1) * DV]).astype(BF16)
    zgs_ref[...] = _dot(xn, wg_ref[...])
    zg_ref[0] = zgs_ref[...].T[:GATE_ROWS, :]


def _const_spec(shape):
    nd = len(shape)
    return pl.BlockSpec(shape, lambda b, i: (0,) * nd, pipeline_mode=pl.Buffered(1))


def _inproj(x, mods, mod_row, g, wqk, wv, wg, cw, cb, wo=None, wm=None, gn=None, tm=512):
    bsz, s, d = x.shape
    ctx_mode = wo is None
    tm = min(tm, s)
    nh = s // HALO
    r = tm // HALO
    wq = wqk.shape[1]
    in_specs = [pl.BlockSpec((1, tm, d), lambda b, i: (b, i, 0)),
                pl.BlockSpec((1, HALO, d), lambda b, i: (b, jnp.maximum(i * r - 1, 0), 0)),
                pl.BlockSpec((1, HALO, d), lambda b, i: (b, jnp.minimum((i + 1) * r, nh - 1), 0)),
                pl.BlockSpec((1, N_MOD, d), lambda b, i: (mod_row(b), 0, 0)),
                _const_spec((1, d)), _const_spec(wqk.shape), _const_spec(wv.shape),
                _const_spec(wg.shape), _const_spec(cw.shape), _const_spec(cb.shape)]
    args = [x, x, x, mods, g.reshape(1, d), wqk, wv, wg, cw, cb]

    def tok(width, dtype):
        return (pl.BlockSpec((1, tm, width), lambda b, i: (b, i, 0)),
                jax.ShapeDtypeStruct((bsz, s, width), dtype))

    kt = (pl.BlockSpec((1, HEADS, tm // CHUNK, DK, CHUNK), lambda b, i: (b, 0, i, 0, 0)),
          jax.ShapeDtypeStruct((bsz, HEADS, s // CHUNK, DK, CHUNK), BF16))
    zgt = (pl.BlockSpec((1, GATE_ROWS, tm), lambda b, i: (b, 0, i)),
           jax.ShapeDtypeStruct((bsz, GATE_ROWS, s), F32))
    def heads(width):
        return (pl.BlockSpec((1, HEADS, tm, width), lambda b, i: (b, 0, i, 0)),
                jax.ShapeDtypeStruct((bsz, HEADS, s, width), BF16))

    if ctx_mode:
        outs = [kt, heads(DV), zgt]
    else:
        in_specs += [_const_spec(wo.shape), _const_spec(wm.shape), _const_spec(gn.shape)]
        args += [wo, wm, gn]
        outs = [heads(DK), kt, heads(DV), zgt,
                heads(DV), tok(wm.shape[1] // 2, BF16), tok(wm.shape[1] // 2, BF16)]
    return pl.pallas_call(
        functools.partial(_inproj_kernel, ctx_mode=ctx_mode, tm=tm),
        grid=(bsz, s // tm),
        in_specs=in_specs,
        out_specs=[o[0] for o in outs],
        out_shape=[o[1] for o in outs],
        scratch_shapes=[pltpu.VMEM((tm + 2 * HALO, COL_TILE), F32), pltpu.VMEM((tm + 2 * HALO, d), BF16),
                        pltpu.VMEM((tm, GATE_PAD), F32)]
        + ([] if ctx_mode else [pltpu.VMEM((tm, wm.shape[1] // 2), F32)]),
        compiler_params=pltpu.CompilerParams(dimension_semantics=("parallel", "parallel"),
                                             vmem_limit_bytes=VMEM_LIMIT),
        name="inproj_ctx" if ctx_mode else "inproj",
    )(*args)


def _lane_scan(x, op, ident, reverse):
    lane = lax.broadcasted_iota(jnp.int32, x.shape, 1)
    n = x.shape[1]
    k = 1
    while k < n:
        if reverse:
            x = op(x, jnp.where(lane < n - k, pltpu.roll(x, n - k, axis=1), ident))
        else:
            x = op(x, jnp.where(lane >= k, pltpu.roll(x, k, axis=1), ident))
        k *= 2
    return x


def _mlstm_kernel(q_ref, kt_ref, v_ref, o_ref, g_ref, ktc_ref, vc_ref, br_ref, ng_ref, out_ref,
                  e_ref, col_ref, tot_ref, emax_ref, rhs_ref, ms_ref, c_ref, m_ref, lhs_ref, floor_ref):
    s = q_ref.shape[2]
    nc = s // CHUNK
    ncc = vc_ref.shape[2] // CHUNK
    row = lax.broadcasted_iota(jnp.int32, (CHUNK, CHUNK), 0)
    col = lax.broadcasted_iota(jnp.int32, (CHUNK, CHUNK), 1)
    eye = row == col
    ones = jnp.ones((CHUNK, NORM_COLS), BF16)
    sel_r = lax.broadcasted_iota(jnp.int32, (3 * CHUNK, 2 * NORM_COLS), 0)
    sel_c = lax.broadcasted_iota(jnp.int32, (3 * CHUNK, 2 * NORM_COLS), 1)
    sel = jnp.where((sel_r < CHUNK) == (sel_c < NORM_COLS), 1.0, 0.0).astype(BF16)

    for d in range(2):
        logi = g_ref[0, 2 * d, 0] + br_ref[0, 2 * d]
        logf = _log_sigmoid(g_ref[0, 2 * d + 1, 0] + br_ref[0, 2 * d + 1])
        b = _lane_scan(logf, jnp.add, 0.0, d == 1)
        e = logi - b
        e_ref[d] = e
        b_hi = b.astype(BF16).astype(F32)
        col_ref[d, 0] = _lane_scan(e, jnp.maximum, -jnp.inf, d == 1).astype(BF16).astype(F32)
        col_ref[d, 1] = b_hi
        col_ref[d, 2] = (b - b_hi).astype(BF16).astype(F32)
        tot_ref[d] = jnp.sum(logf, axis=-1, keepdims=True)
        emax_ref[d] = jnp.max(e, axis=-1, keepdims=True)

    def scan_step(d, r, kt, v):
        e_r = e_ref[d, pl.ds(r, 1), :]
        emax = emax_ref[d, pl.ds(r, 1), :]
        kw = (kt.astype(F32) * jnp.exp(e_r - emax)).astype(BF16)
        v_aug = jnp.concatenate([v, ones], axis=1)
        upd = _dot(kw, v_aug)
        m = m_ref[d]
        c = c_ref[d]
        mx = jnp.maximum(m, emax)
        c_ref[d] = jnp.exp(m - mx) * c + jnp.exp(emax - mx) * upd
        m_ref[d] = tot_ref[d, pl.ds(r, 1), :] + mx
        return c.astype(BF16), m, v_aug

    c_ref[...] = jnp.zeros_like(c_ref)
    m_ref[...] = jnp.zeros_like(m_ref)
    for t in range(ncc):
        for d in range(2):
            ci = t if d == 0 else ncc - 1 - t
            scan_step(d, ci, ktc_ref[0, 0, ci], vc_ref[0, 0, ci * CHUNK:(ci + 1) * CHUNK, :])

    def scan_body(t, carry):
        for d in range(2):
            ci = t if d == 0 else nc - 1 - t
            rows = pl.ds(pl.multiple_of(ci * CHUNK, CHUNK), CHUNK)
            c_in, m_in, v_aug = scan_step(d, CTX_ROWS + ci, kt_ref[0, 0, ci], v_ref[0, 0, rows, :])
            rhs_ref[d, ci, 0:DK, :] = c_in
            rhs_ref[d, ci, DK:, :] = v_aug
            ms_ref[d, ci] = m_in
        return carry

    lax.fori_loop(0, nc, scan_body, 0, unroll=4)

    def weights(ci, slot):
        q = q_ref[0, 0, pl.ds(pl.multiple_of(ci * CHUNK, CHUNK), CHUNK), :]
        qf = q.astype(F32)
        qk = _dot(q, kt_ref[0, 0, ci])
        r = pl.ds(CTX_ROWS + ci, 1)
        for d in range(2):
            inc = (col <= row) if d == 0 else (col >= row)
            e_m = jnp.where(inc, e_ref[d, r, :], -jnp.inf)
            diag = jnp.concatenate([jnp.where(eye, col_ref[d, i, r, :], 0.0) for i in range(3)], axis=1)
            rep = _dot(diag.astype(BF16), sel)
            m = ms_ref[d, ci]
            mm = jnp.maximum(m, rep[:, :NORM_COLS])
            p = (qk * jnp.exp(e_m - mm)).astype(BF16)
            qa = (qf * jnp.exp(m - mm)).astype(BF16)
            lhs_ref[slot, d] = jnp.concatenate([qa, p], axis=1)
            floor_ref[slot, d] = jnp.exp(-(rep[:, NORM_COLS:] + mm))

    def apply(ci, slot):
        rows = pl.ds(pl.multiple_of(ci * CHUNK, CHUNK), CHUNK)
        h = None
        for d in range(2):
            comb = _dot(lhs_ref[slot, d], rhs_ref[d, ci])
            inv = 1.0 / jnp.maximum(jnp.abs(comb[:, DV:]), floor_ref[slot, d])
            hd = comb[:, :DV] * jnp.concatenate([inv] * (DV // NORM_COLS), axis=1)
            h = hd if h is None else h + hd
        out_ref[0, 0, rows, :] = (_rms(h, ng_ref[...]) * o_ref[0, 0, rows, :].astype(F32)).astype(BF16)

    def out_body(ci, carry):
        apply(ci - 1, (ci - 1) % 2)
        weights(ci, ci % 2)
        return carry

    weights(0, 0)
    lax.fori_loop(1, nc, out_body, 0, unroll=4)
    apply(nc - 1, (nc - 1) % 2)


def _mlstm(q, kt, v, o, g, ktc, vc, bias, norm_g):
    bsz, _, s, _ = q.shape
    sc = vc.shape[2]
    nc, ncc = s // CHUNK, sc // CHUNK
    nr = g.shape[3]
    return pl.pallas_call(
        _mlstm_kernel,
        grid=(bsz, HEADS),
        in_specs=[pl.BlockSpec((1, 1, s, DK), lambda b, h: (b, h, 0, 0)),
                  pl.BlockSpec((1, 1, nc, DK, CHUNK), lambda b, h: (b, h, 0, 0, 0)),
                  pl.BlockSpec((1, 1, s, DV), lambda b, h: (b, h, 0, 0)),
                  pl.BlockSpec((1, 1, s, DV), lambda b, h: (b, h, 0, 0)),
                  pl.BlockSpec((1, 4, 1, nr, CHUNK), lambda b, h: (b, 0, h, 0, 0)),
                  pl.BlockSpec((1, 1, ncc, DK, CHUNK), lambda b, h: (b, h, 0, 0, 0)),
                  pl.BlockSpec((1, 1, sc, DV), lambda b, h: (b, h, 0, 0)),
                  pl.BlockSpec((1, 4, 1, 1), lambda b, h: (h, 0, 0, 0)),
                  pl.BlockSpec((1, DV), lambda b, h: (0, h))],
        out_specs=pl.BlockSpec((1, 1, s, DV), lambda b, h: (b, h, 0, 0)),
        out_shape=jax.ShapeDtypeStruct((bsz, HEADS, s, DV), BF16),
        scratch_shapes=[pltpu.VMEM((2, nr, CHUNK), F32), pltpu.VMEM((2, 3, nr, CHUNK), F32),
                        pltpu.VMEM((2, nr, 1), F32), pltpu.VMEM((2, nr, 1), F32),
                        pltpu.VMEM((2, nc, DK + CHUNK, DV + NORM_COLS), BF16), pltpu.VMEM((2, nc, 1, 1), F32),
                        pltpu.VMEM((2, DK, DV + NORM_COLS), F32), pltpu.VMEM((2, 1, 1), F32),
                        pltpu.VMEM((2, 2, CHUNK, DK + CHUNK), BF16), pltpu.VMEM((2, 2, CHUNK, NORM_COLS), F32)],
        compiler_params=pltpu.CompilerParams(dimension_semantics=("parallel", "parallel"),
                                             vmem_limit_bytes=VMEM_LIMIT),
        name="mlstm",
    )(q, kt, v, o, g, ktc, vc, bias, norm_g)


def _outproj_kernel(x_ref, mod_ref, hx_ref, u_ref, vn_ref, gw_ref, gb_ref, wt_ref, wb_ref, o_ref, gx_ref):
    tm = x_ref.shape[1]
    for ci in range(tm // CHUNK):
        rows = slice(ci * CHUNK, (ci + 1) * CHUNK)
        for g in range(GROUPS):
            cols = slice(g * GD, (g + 1) * GD)
            sp = _dot(gw_ref[g], vn_ref[0, rows, cols]) + gb_ref[:, cols]
            gx_ref[rows, cols] = (u_ref[0, rows, cols].astype(F32) * sp).astype(BF16)
    y = _dot(gx_ref[...], wb_ref[...])
    for h in range(HEADS):
        y = y + _dot(hx_ref[0, h], wt_ref[h * DV:(h + 1) * DV, :])
    o_ref[0] = x_ref[0] + mod_ref[0, 5:6, :] * y


def _outproj(x, mods, hx, u, vn, gw, gb, wt, wb, tm=512):
    bsz, s, d = x.shape
    w = u.shape[2]
    return pl.pallas_call(
        _outproj_kernel,
        grid=(bsz, s // tm),
        in_specs=[pl.BlockSpec((1, tm, d), lambda b, i: (b, i, 0)),
                  pl.BlockSpec((1, N_MOD, d), lambda b, i: (b, 0, 0)),
                  pl.BlockSpec((1, HEADS, tm, DV), lambda b, i: (b, 0, i, 0)),
                  pl.BlockSpec((1, tm, w), lambda b, i: (b, i, 0)),
                  pl.BlockSpec((1, tm, w), lambda b, i: (b, i, 0)),
                  _const_spec(gw.shape), _const_spec(gb.shape),
                  _const_spec(wt.shape), _const_spec(wb.shape)],
        out_specs=pl.BlockSpec((1, tm, d), lambda b, i: (b, i, 0)),
        out_shape=jax.ShapeDtypeStruct((bsz, s, d), F32),
        scratch_shapes=[pltpu.VMEM((tm, w), BF16)],
        compiler_params=pltpu.CompilerParams(dimension_semantics=("parallel", "parallel"),
                                             vmem_limit_bytes=VMEM_LIMIT),
        name="outproj",
    )(x, mods, hx, u, vn, gw, gb, wt, wb)


def _gate_rows(zgc, zg):
    def rows(z):
        bsz, _, t = z.shape
        return z.reshape(bsz, 4, HEADS, t // CHUNK, CHUNK)
    gc = rows(zgc)
    assert gc.shape[3] <= CTX_ROWS
    gc = jnp.pad(gc, ((0, 0), (0, 0), (0, 0), (0, CTX_ROWS - gc.shape[3]), (0, 0)))
    return jnp.concatenate([gc, rows(zg)], axis=3)


def kernel(x, c, ctx, c_ctx, w_ada, b_ada, norm_ffn1, w_ffn1_in, w_ffn1_out, norm_mix, w_in, conv_w, conv_b, b_igate, b_fgate, mlstm_norm, gmlp_norm, gmlp_w, gmlp_b, w_out, norm_ffn2, w_ffn2_in, w_ffn2_out, final_norm):
    bsz, s, d = x.shape
    depth = w_ada.shape[0]
    assert depth == 1, "context-stream update for non-final layers is not implemented"
    l = 0
    qk_w = 2 * HEADS * DK
    mw = HEADS * DV
    c_qk, c_v, c_o = qk_w, qk_w + mw, qk_w + 2 * mw
    c_gate = c_o + 4 * HEADS

    lat_row = lambda b: b
    ctx_row = lambda b: bsz

    rows = -(-(bsz + 1) // 8) * 8
    cond = jnp.zeros((rows, d), F32).at[:bsz].set(c).at[bsz].set(c_ctx)
    mods = _adaln(cond, w_ada[l], b_ada[l]).reshape(rows, N_MOD, d)

    w1i, w1o = w_ffn1_in[l].astype(BF16), w_ffn1_out[l].astype(BF16)
    w2i, w2o = w_ffn2_in[l].astype(BF16), w_ffn2_out[l].astype(BF16)
    wi = w_in[l]
    wqk, wv, wo = (wi[:, a:b].astype(BF16) for a, b in ((0, c_qk), (c_qk, c_v), (c_v, c_o)))
    wg = jnp.pad(wi[:, c_o:c_gate].astype(BF16), ((0, 0), (0, GATE_PAD - 4 * HEADS)))
    wm = wi[:, c_gate:].astype(BF16)
    cw = jnp.pad(conv_w[l], ((0, 8 - CONV_K), (0, 0)))
    cb = conv_b[l].reshape(1, qk_w)
    wout = w_out[l].astype(BF16)

    ffn_tm = 1024
    lat_tile_row = lambda t: t // (s // ffn_tm)
    x = _ffn(x.reshape(bsz * s, d), mods, lat_tile_row, 0, norm_ffn1[l], w1i, w1o, tm=ffn_tm).reshape(bsz, s, d)
    ctx = _ffn(ctx.reshape(-1, d), mods, ctx_row, 0, norm_ffn1[l], w1i, w1o, tm=ffn_tm).reshape(ctx.shape)

    q, kt, v, zg, o, u, vn = _inproj(x, mods, lat_row, norm_mix[l], wqk, wv, wg, cw, cb,
                                     wo, wm, gmlp_norm[l].reshape(1, -1))
    ktc, vc, zgc = _inproj(ctx, mods, ctx_row, norm_mix[l], wqk[:, qk_w // 2:], wv, wg,
                           cw[:, qk_w // 2:], cb[:, qk_w // 2:])

    bias = jnp.stack([b_igate[l][0], b_fgate[l][0], b_igate[l][1], b_fgate[l][1]], axis=-1)
    hx = _mlstm(q, kt, v, o, _gate_rows(zgc, zg), ktc, vc, bias.reshape(HEADS, 4, 1, 1),
                mlstm_norm[l].reshape(1, mw))

    gb = jnp.repeat(gmlp_b[l].T, GD, axis=1)
    x = _outproj(x, mods, hx, u, vn, gmlp_w[l].astype(BF16), gb, wout[:mw], wout[mw:])

    return _ffn(x.reshape(bsz * s, d), mods, lat_tile_row, 6, norm_ffn2[l], w2i, w2o, final_g=final_norm,
                tm=ffn_tm).reshape(bsz, s, d)
```

```python
import functools

import jax
import jax.numpy as jnp
from jax import lax
from jax.experimental import pallas as pl
from jax.experimental.pallas import tpu as pltpu

F32 = jnp.float32
BF16 = jnp.bfloat16

EPS = 1e-6
N_MOD = 9
HEADS = 4
DK = 128
DV = 256
CHUNK = 128
GROUPS = 8
GD = 128
CONV_K = 5
HALO = 16
COL_TILE = 256
LOOKAHEAD = 1
GATE_PAD = 128
GATE_ROWS = 4 * HEADS
ROW_SLAB = 16
CTX_ROWS = 8
NORM_COLS = 128

VMEM_LIMIT = 56 * 1024 * 1024


def _sigmoid(x):
    return 1.0 / (1.0 + jnp.exp(-x))


def _log_sigmoid(x):
    return jnp.minimum(x, 0.0) - jnp.log(1.0 + jnp.exp(-jnp.abs(x)))


def _gelu_tanh(x):
    c2 = -2.0 * 0.7978845608028654
    return x / (1.0 + jnp.exp(x * (c2 + (c2 * 0.044715) * (x * x))))


def _rms(x, g):
    return x * lax.rsqrt(jnp.mean(x * x, axis=-1, keepdims=True) + EPS) * g


def _norm_mod(x, g, shift, scale):
    return _rms(x, g) * (1.0 + scale) + shift


def _dot(a, b):
    return jnp.dot(a, b, preferred_element_type=F32)


def _ada_kernel(c_ref, w_ref, b_ref, o_ref):
    c = c_ref[...]
    s = (c * _sigmoid(c)).astype(BF16)
    o_ref[...] = _dot(s, w_ref[...].astype(BF16)) + b_ref[...]


def _adaln(cond, w, b, tn=1024):
    rows, d = cond.shape
    n = w.shape[1]
    return pl.pallas_call(
        _ada_kernel,
        grid=(n // tn,),
        in_specs=[pl.BlockSpec((rows, d), lambda j: (0, 0)),
                  pl.BlockSpec((d, tn), lambda j: (0, j)),
                  pl.BlockSpec((1, tn), lambda j: (0, j))],
        out_specs=pl.BlockSpec((rows, tn), lambda j: (0, j)),
        out_shape=jax.ShapeDtypeStruct((rows, n), F32),
        compiler_params=pltpu.CompilerParams(dimension_semantics=("parallel",),
                                             vmem_limit_bytes=VMEM_LIMIT),
        name="adaln",
    )(cond, w, b.reshape(1, n))


def _ffn_kernel(*refs, mod_base, final):
    if final:
        x_ref, mod_ref, g_ref, wg_ref, wu_ref, wo_ref, fn_ref, o_ref, xn_ref, r_ref = refs
    else:
        x_ref, mod_ref, g_ref, wg_ref, wu_ref, wo_ref, o_ref, xn_ref = refs
    f = pl.program_id(2)
    tm = xn_ref.shape[0]

    @pl.when(f == 0)
    def _():
        shift = mod_ref[0, mod_base:mod_base + 1, :]
        gain = g_ref[...] * (1.0 + mod_ref[0, mod_base + 1:mod_base + 2, :])

        def slab(i, carry):
            rows = pl.ds(pl.multiple_of(i * ROW_SLAB, ROW_SLAB), ROW_SLAB)
            x = x_ref[0, rows, :]
            r = lax.rsqrt(jnp.mean(x * x, axis=-1, keepdims=True) + EPS)
            xn_ref[rows, :] = (x * r * gain + shift).astype(BF16)
            o_ref[0, rows, :] = jnp.zeros_like(x)
            return carry

        lax.fori_loop(0, tm // ROW_SLAB, slab, 0, unroll=8)

    xn = xn_ref[...]
    gate = _dot(xn, wg_ref[...])
    up = _dot(xn, wu_ref[...])
    h = (gate * _sigmoid(gate) * up).astype(BF16)
    o_ref[0] += _dot(h, wo_ref[...])

    @pl.when(f == pl.num_programs(2) - 1)
    def _():
        half_gate = 0.5 * mod_ref[0, mod_base + 2:mod_base + 3, :]

        def slab(i, carry):
            rows = pl.ds(pl.multiple_of(i * ROW_SLAB, ROW_SLAB), ROW_SLAB)
            y = x_ref[0, rows, :] + half_gate * o_ref[0, rows, :]
            o_ref[0, rows, :] = y
            if final:
                r_ref[rows, :] = lax.rsqrt(jnp.mean(y * y, axis=-1, keepdims=True) + EPS)
            return carry

        lax.fori_loop(0, tm // ROW_SLAB, slab, 0, unroll=8)

        if final:
            def scale(i, carry):
                rows = pl.ds(pl.multiple_of(i * ROW_SLAB, ROW_SLAB), ROW_SLAB)
                o_ref[0, rows, :] = o_ref[0, rows, :] * r_ref[rows, :] * fn_ref[...]
                return carry

            lax.fori_loop(0, tm // ROW_SLAB, scale, 0, unroll=8)


def _ffn(x, mods, mod_row, mod_base, g, w_in, w_out, final_g=None, tm=1024, tf=512):
    bsz, s, d = x.shape
    ff = w_out.shape[0]
    tm = min(tm, s)
    nf = ff // tf
    final = final_g is not None
    in_specs = [pl.BlockSpec((1, tm, d), lambda b, i, f: (b, i, 0)),
                pl.BlockSpec((1, N_MOD, d), lambda b, i, f: (mod_row(b), 0, 0)),
                pl.BlockSpec((1, d), lambda b, i, f: (0, 0)),
                pl.BlockSpec((d, tf), lambda b, i, f: (0, f)),
                pl.BlockSpec((d, tf), lambda b, i, f: (0, nf + f)),
                pl.BlockSpec((tf, d), lambda b, i, f: (f, 0))]
    args = [x, mods, g.reshape(1, d), w_in, w_in, w_out]
    if final:
        in_specs.append(pl.BlockSpec((1, d), lambda b, i, f: (0, 0)))
        args.append(final_g.reshape(1, d))
    return pl.pallas_call(
        functools.partial(_ffn_kernel, mod_base=mod_base, final=final),
        grid=(bsz, s // tm, nf),
        in_specs=in_specs,
        out_specs=pl.BlockSpec((1, tm, d), lambda b, i, f: (b, i, 0)),
        out_shape=jax.ShapeDtypeStruct((bsz, s, d), F32),
        scratch_shapes=[pltpu.VMEM((tm, d), BF16)] + ([pltpu.VMEM((tm, 1), F32)] if final else []),
        compiler_params=pltpu.CompilerParams(
            dimension_semantics=("parallel", "parallel", "arbitrary"),
            vmem_limit_bytes=VMEM_LIMIT),
        name="ffn_final" if final else "ffn",
    )(*args)


def _inproj_kernel(*refs, ctx_mode, tm):
    if ctx_mode:
        (x_ref, xp_ref, xq_ref, mod_ref, g_ref, wqk_ref, wv_ref, wg_ref, cw_ref, cb_ref,
         k_ref, v_ref, zg_ref, zs_ref, xn_ref, zgs_ref) = refs
    else:
        (x_ref, xp_ref, xq_ref, mod_ref, g_ref, wqk_ref, wv_ref, wg_ref, cw_ref, cb_ref,
         wo_ref, wm_ref, gn_ref,
         q_ref, k_ref, v_ref, zg_ref, o_ref, u_ref, vn_ref, zs_ref, xn_ref, zgs_ref, gv_ref) = refs
    i = pl.program_id(1)
    g = g_ref[...]
    shift = mod_ref[0, 3:4, :]
    gain = g * (1.0 + mod_ref[0, 4:5, :])

    def norm_rows(x):
        r = lax.rsqrt(jnp.mean(x * x, axis=-1, keepdims=True) + EPS)
        return (x * r * gain + shift).astype(BF16)

    xn_ref[0:HALO, :] = norm_rows(xp_ref[0])
    xn_ref[HALO + tm:, :] = norm_rows(xq_ref[0])

    def slab(t, carry):
        rows = pl.ds(pl.multiple_of(t * ROW_SLAB, ROW_SLAB), ROW_SLAB)
        xn_ref[pl.ds(pl.multiple_of(HALO + t * ROW_SLAB, ROW_SLAB), ROW_SLAB), :] = norm_rows(x_ref[0, rows, :])
        return carry

    lax.fori_loop(0, tm // ROW_SLAB, slab, 0, unroll=8)
    xn = xn_ref[HALO:HALO + tm, :]

    def col(t):
        return slice(t * COL_TILE, (t + 1) * COL_TILE)

    tasks = []
    ssq = [jnp.zeros((tm, 1), F32)]

    if not ctx_mode:
        gw = wm_ref.shape[1] // 2

        def gv_epilogue(z, t):
            gv = _gelu_tanh(z)
            gv_ref[:, col(t)] = gv
            ssq[0] = ssq[0] + jnp.sum(gv * gv, axis=-1, keepdims=True)

        def u_epilogue(z, t):
            u_ref[0, :, col(t)] = _gelu_tanh(z).astype(BF16)

        for t in range(gw // COL_TILE):
            tasks.append((functools.partial(lambda t: _dot(xn, wm_ref[:, gw + t * COL_TILE:gw + (t + 1) * COL_TILE]), t),
                          functools.partial(gv_epilogue, t=t)))
        for t in range(gw // COL_TILE):
            tasks.append((functools.partial(lambda t: _dot(xn, wm_ref[:, col(t)]), t),
                          functools.partial(u_epilogue, t=t)))

    wq = wqk_ref.shape[1]
    k_off = 0 if ctx_mode else wq // 2

    def qk_epilogue(z, t):
        zs = zs_ref.at[t % 2]
        zs[...] = z
        zs[0:HALO, :] = jnp.where(i == 0, 0.0, z[0:HALO])
        zs[HALO + tm:, :] = jnp.where(i == pl.num_programs(1) - 1, 0.0, z[HALO + tm:])
        conv = cb_ref[:, col(t)]
        for j in range(CONV_K):
            off = HALO - CONV_K // 2 + j
            conv = conv + zs[off:off + tm, :] * cw_ref[j:j + 1, col(t)]
        qk = conv * _sigmoid(conv)
        for hh in range(COL_TILE // DK):
            c0 = t * COL_TILE + hh * DK
            if c0 < k_off:
                q_ref[0, c0 // DK] = (qk[:, hh * DK:(hh + 1) * DK] * (DK ** -0.5)).astype(BF16)
            else:
                for cc in range(tm // CHUNK):
                    k_ref[0, (c0 - k_off) // DK, cc] = (
                        qk[cc * CHUNK:(cc + 1) * CHUNK, hh * DK:(hh + 1) * DK].T.astype(BF16))

    for t in range(wq // COL_TILE):
        tasks.append((functools.partial(lambda t: _dot(xn_ref[...], wqk_ref[:, col(t)]), t),
                      functools.partial(qk_epilogue, t=t)))

    if not ctx_mode:
        def o_epilogue(z, h):
            o_ref[0, h] = _sigmoid(z).astype(BF16)

        for h in range(HEADS):
            tasks.append((functools.partial(lambda h: _dot(xn, wo_ref[:, h * DV:(h + 1) * DV]), h),
                          functools.partial(o_epilogue, h=h)))

    def v_epilogue(z, h):
        if h == 0 and not ctx_mode:
            r = lax.rsqrt(ssq[0] * (1.0 / gw) + EPS)
            vn_ref[0] = (gv_ref[...] * r * gn_ref[...]).astype(BF16)
        v_ref[0, h] = z.astype(BF16)

    for h in range(HEADS):
        tasks.append((functools.partial(lambda h: _dot(xn, wv_ref[:, h * DV:(h + 1) * DV]), h),
                      functools.partial(v_epilogue, h=h)))

    def gate_epilogue(z):
        zgs_ref[...] = z
        zg_ref[0] = zgs_ref[...].T[:GATE_ROWS, :]

    tasks.append((lambda: _dot(xn, wg_ref[...]), gate_epilogue))

    pending = [product() for product, _ in tasks[:LOOKAHEAD]]
    for n, (_, epilogue) in enumerate(tasks):
        if n + LOOKAHEAD < len(tasks):
            pending.append(tasks[n + LOOKAHEAD][0]())
        epilogue(pending.pop(0))


def _const_spec(shape):
    nd = len(shape)
    return pl.BlockSpec(shape, lambda b, i: (0,) * nd, pipeline_mode=pl.Buffered(1))


def _inproj(x, mods, mod_row, g, wqk, wv, wg, cw, cb, wo=None, wm=None, gn=None, tm=512):
    bsz, s, d = x.shape
    ctx_mode = wo is None
    tm = min(tm, s)
    nh = s // HALO
    r = tm // HALO
    wq = wqk.shape[1]
    in_specs = [pl.BlockSpec((1, tm, d), lambda b, i: (b, i, 0)),
                pl.BlockSpec((1, HALO, d), lambda b, i: (b, jnp.maximum(i * r - 1, 0), 0)),
                pl.BlockSpec((1, HALO, d), lambda b, i: (b, jnp.minimum((i + 1) * r, nh - 1), 0)),
                pl.BlockSpec((1, N_MOD, d), lambda b, i: (mod_row(b), 0, 0)),
                _const_spec((1, d)), _const_spec(wqk.shape), _const_spec(wv.shape),
                _const_spec(wg.shape), _const_spec(cw.shape), _const_spec(cb.shape)]
    args = [x, x, x, mods, g.reshape(1, d), wqk, wv, wg, cw, cb]

    def tok(width, dtype):
        return (pl.BlockSpec((1, tm, width), lambda b, i: (b, i, 0)),
                jax.ShapeDtypeStruct((bsz, s, width), dtype))

    kt = (pl.BlockSpec((1, HEADS, tm // CHUNK, DK, CHUNK), lambda b, i: (b, 0, i, 0, 0)),
          jax.ShapeDtypeStruct((bsz, HEADS, s // CHUNK, DK, CHUNK), BF16))
    zgt = (pl.BlockSpec((1, GATE_ROWS, tm), lambda b, i: (b, 0, i)),
           jax.ShapeDtypeStruct((bsz, GATE_ROWS, s), F32))
    def heads(width):
        return (pl.BlockSpec((1, HEADS, tm, width), lambda b, i: (b, 0, i, 0)),
                jax.ShapeDtypeStruct((bsz, HEADS, s, width), BF16))

    if ctx_mode:
        outs = [kt, heads(DV), zgt]
    else:
        in_specs += [_const_spec(wo.shape), _const_spec(wm.shape), _const_spec(gn.shape)]
        args += [wo, wm, gn]
        outs = [heads(DK), kt, heads(DV), zgt,
                heads(DV), tok(wm.shape[1] // 2, BF16), tok(wm.shape[1] // 2, BF16)]
    return pl.pallas_call(
        functools.partial(_inproj_kernel, ctx_mode=ctx_mode, tm=tm),
        grid=(bsz, s // tm),
        in_specs=in_specs,
        out_specs=[o[0] for o in outs],
        out_shape=[o[1] for o in outs],
        scratch_shapes=[pltpu.VMEM((2, tm + 2 * HALO, COL_TILE), F32), pltpu.VMEM((tm + 2 * HALO, d), BF16),
                        pltpu.VMEM((tm, GATE_PAD), F32)]
        + ([] if ctx_mode else [pltpu.VMEM((tm, wm.shape[1] // 2), F32)]),
        compiler_params=pltpu.CompilerParams(dimension_semantics=("parallel", "parallel"),
                                             vmem_limit_bytes=VMEM_LIMIT),
        name="inproj_ctx" if ctx_mode else "inproj",
    )(*args)


def _lane_scan(x, op, ident, reverse):
    lane = lax.broadcasted_iota(jnp.int32, x.shape, 1)
    n = x.shape[1]
    k = 1
    while k < n:
        if reverse:
            x = op(x, jnp.where(lane < n - k, pltpu.roll(x, n - k, axis=1), ident))
        else:
            x = op(x, jnp.where(lane >= k, pltpu.roll(x, k, axis=1), ident))
        k *= 2
    return x


def _mlstm_kernel(q_ref, kt_ref, v_ref, o_ref, g_ref, ktc_ref, vc_ref, br_ref, ng_ref, out_ref,
                  e_ref, col_ref, tot_ref, emax_ref, rhs_ref, ms_ref, c_ref, m_ref, lhs_ref, floor_ref):
    s = q_ref.shape[2]
    nc = s // CHUNK
    ncc = vc_ref.shape[2] // CHUNK
    row = lax.broadcasted_iota(jnp.int32, (CHUNK, CHUNK), 0)
    col = lax.broadcasted_iota(jnp.int32, (CHUNK, CHUNK), 1)
    eye = row == col
    ones = jnp.ones((CHUNK, NORM_COLS), BF16)
    sel_r = lax.broadcasted_iota(jnp.int32, (3 * CHUNK, 2 * NORM_COLS), 0)
    sel_c = lax.broadcasted_iota(jnp.int32, (3 * CHUNK, 2 * NORM_COLS), 1)
    sel = jnp.where((sel_r < CHUNK) == (sel_c < NORM_COLS), 1.0, 0.0).astype(BF16)

    for d in range(2):
        logi = g_ref[0, 2 * d, 0] + br_ref[0, 2 * d]
        logf = _log_sigmoid(g_ref[0, 2 * d + 1, 0] + br_ref[0, 2 * d + 1])
        b = _lane_scan(logf, jnp.add, 0.0, d == 1)
        e = logi - b
        e_ref[d] = e
        b_hi = b.astype(BF16).astype(F32)
        col_ref[d, 0] = _lane_scan(e, jnp.maximum, -jnp.inf, d == 1).astype(BF16).astype(F32)
        col_ref[d, 1] = b_hi
        col_ref[d, 2] = (b - b_hi).astype(BF16).astype(F32)
        tot_ref[d] = jnp.sum(logf, axis=-1, keepdims=True)
        emax_ref[d] = jnp.max(e, axis=-1, keepdims=True)

    def scan_step(d, r, kt, v):
        e_r = e_ref[d, pl.ds(r, 1), :]
        emax = emax_ref[d, pl.ds(r, 1), :]
        kw = (kt.astype(F32) * jnp.exp(e_r - emax)).astype(BF16)
        v_aug = jnp.concatenate([v, ones], axis=1)
        upd = _dot(kw, v_aug)
        m = m_ref[d]
        c = c_ref[d]
        mx = jnp.maximum(m, emax)
        c_ref[d] = jnp.exp(m - mx) * c + jnp.exp(emax - mx) * upd
        m_ref[d] = tot_ref[d, pl.ds(r, 1), :] + mx
        return c.astype(BF16), m, v_aug

    c_ref[...] = jnp.zeros_like(c_ref)
    m_ref[...] = jnp.zeros_like(m_ref)
    for t in range(ncc):
        for d in range(2):
            ci = t if d == 0 else ncc - 1 - t
            scan_step(d, ci, ktc_ref[0, 0, ci], vc_ref[0, 0, ci * CHUNK:(ci + 1) * CHUNK, :])

    def scan_body(t, carry):
        for d in range(2):
            ci = t if d == 0 else nc - 1 - t
            rows = pl.ds(pl.multiple_of(ci * CHUNK, CHUNK), CHUNK)
            c_in, m_in, v_aug = scan_step(d, CTX_ROWS + ci, kt_ref[0, 0, ci], v_ref[0, 0, rows, :])
            rhs_ref[d, ci, 0:DK, :] = c_in
            rhs_ref[d, ci, DK:, :] = v_aug
            ms_ref[d, ci] = m_in
        return carry

    lax.fori_loop(0, nc, scan_body, 0, unroll=4)

    def weights(ci, slot):
        q = q_ref[0, 0, pl.ds(pl.multiple_of(ci * CHUNK, CHUNK), CHUNK), :]
        qf = q.astype(F32)
        qk = _dot(q, kt_ref[0, 0, ci])
        r = pl.ds(CTX_ROWS + ci, 1)
        for d in range(2):
            inc = (col <= row) if d == 0 else (col >= row)
            e_m = jnp.where(inc, e_ref[d, r, :], -jnp.inf)
            diag = jnp.concatenate([jnp.where(eye, col_ref[d, i, r, :], 0.0) for i in range(3)], axis=1)
            rep = _dot(diag.astype(BF16), sel)
            m = ms_ref[d, ci]
            mm = jnp.maximum(m, rep[:, :NORM_COLS])
            p = (qk * jnp.exp(e_m - mm)).astype(BF16)
            qa = (qf * jnp.exp(m - mm)).astype(BF16)
            lhs_ref[slot, d] = jnp.concatenate([qa, p], axis=1)
            floor_ref[slot, d] = jnp.exp(-(rep[:, NORM_COLS:] + mm))

    def apply(ci, slot):
        rows = pl.ds(pl.multiple_of(ci * CHUNK, CHUNK), CHUNK)
        h = None
        for d in range(2):
            comb = _dot(lhs_ref[slot, d], rhs_ref[d, ci])
            inv = 1.0 / jnp.maximum(jnp.abs(comb[:, DV:]), floor_ref[slot, d])
            hd = comb[:, :DV] * jnp.concatenate([inv] * (DV // NORM_COLS), axis=1)
            h = hd if h is None else h + hd
        out_ref[0, 0, rows, :] = (_rms(h, ng_ref[...]) * o_ref[0, 0, rows, :].astype(F32)).astype(BF16)

    def out_body(ci, carry):
        apply(ci - 1, (ci - 1) % 2)
        weights(jnp.minimum(ci, nc - 1), ci % 2)
        return carry

    weights(0, 0)
    lax.fori_loop(1, nc + 1, out_body, 0, unroll=4)


def _mlstm(q, kt, v, o, g, ktc, vc, bias, norm_g):
    bsz, _, s, _ = q.shape
    sc = vc.shape[2]
    nc, ncc = s // CHUNK, sc // CHUNK
    nr = g.shape[3]
    return pl.pallas_call(
        _mlstm_kernel,
        grid=(bsz, HEADS),
        in_specs=[pl.BlockSpec((1, 1, s, DK), lambda b, h: (b, h, 0, 0)),
                  pl.BlockSpec((1, 1, nc, DK, CHUNK), lambda b, h: (b, h, 0, 0, 0)),
                  pl.BlockSpec((1, 1, s, DV), lambda b, h: (b, h, 0, 0)),
                  pl.BlockSpec((1, 1, s, DV), lambda b, h: (b, h, 0, 0)),
                  pl.BlockSpec((1, 4, 1, nr, CHUNK), lambda b, h: (b, 0, h, 0, 0)),
                  pl.BlockSpec((1, 1, ncc, DK, CHUNK), lambda b, h: (b, h, 0, 0, 0)),
                  pl.BlockSpec((1, 1, sc, DV), lambda b, h: (b, h, 0, 0)),
                  pl.BlockSpec((1, 4, 1, 1), lambda b, h: (h, 0, 0, 0)),
                  pl.BlockSpec((1, DV), lambda b, h: (0, h))],
        out_specs=pl.BlockSpec((1, 1, s, DV), lambda b, h: (b, h, 0, 0)),
        out_shape=jax.ShapeDtypeStruct((bsz, HEADS, s, DV), BF16),
        scratch_shapes=[pltpu.VMEM((2, nr, CHUNK), F32), pltpu.VMEM((2, 3, nr, CHUNK), F32),
                        pltpu.VMEM((2, nr, 1), F32), pltpu.VMEM((2, nr, 1), F32),
                        pltpu.VMEM((2, nc, DK + CHUNK, DV + NORM_COLS), BF16), pltpu.VMEM((2, nc, 1, 1), F32),
                        pltpu.VMEM((2, DK, DV + NORM_COLS), F32), pltpu.VMEM((2, 1, 1), F32),
                        pltpu.VMEM((2, 2, CHUNK, DK + CHUNK), BF16), pltpu.VMEM((2, 2, CHUNK, NORM_COLS), F32)],
        compiler_params=pltpu.CompilerParams(dimension_semantics=("parallel", "parallel"),
                                             vmem_limit_bytes=VMEM_LIMIT),
        name="mlstm",
    )(q, kt, v, o, g, ktc, vc, bias, norm_g)


def _outproj_kernel(x_ref, mod_ref, hx_ref, u_ref, vn_ref, gw_ref, gb_ref, wt_ref, wb_ref, o_ref, gx_ref):
    tm = x_ref.shape[1]
    for ci in range(tm // CHUNK):
        rows = slice(ci * CHUNK, (ci + 1) * CHUNK)
        for g in range(GROUPS):
            cols = slice(g * GD, (g + 1) * GD)
            sp = _dot(gw_ref[g], vn_ref[0, rows, cols]) + gb_ref[:, cols]
            gx_ref[rows, cols] = (u_ref[0, rows, cols].astype(F32) * sp).astype(BF16)
    y = _dot(gx_ref[...], wb_ref[...])
    for h in range(HEADS):
        y = y + _dot(hx_ref[0, h], wt_ref[h * DV:(h + 1) * DV, :])
    o_ref[0] = x_ref[0] + mod_ref[0, 5:6, :] * y


def _outproj(x, mods, hx, u, vn, gw, gb, wt, wb, tm=512):
    bsz, s, d = x.shape
    w = u.shape[2]
    return pl.pallas_call(
        _outproj_kernel,
        grid=(bsz, s // tm),
        in_specs=[pl.BlockSpec((1, tm, d), lambda b, i: (b, i, 0)),
                  pl.BlockSpec((1, N_MOD, d), lambda b, i: (b, 0, 0)),
                  pl.BlockSpec((1, HEADS, tm, DV), lambda b, i: (b, 0, i, 0)),
                  pl.BlockSpec((1, tm, w), lambda b, i: (b, i, 0)),
                  pl.BlockSpec((1, tm, w), lambda b, i: (b, i, 0)),
                  _const_spec(gw.shape), _const_spec(gb.shape),
                  _const_spec(wt.shape), _const_spec(wb.shape)],
        out_specs=pl.BlockSpec((1, tm, d), lambda b, i: (b, i, 0)),
        out_shape=jax.ShapeDtypeStruct((bsz, s, d), F32),
        scratch_shapes=[pltpu.VMEM((tm, w), BF16)],
        compiler_params=pltpu.CompilerParams(dimension_semantics=("parallel", "parallel"),
                                             vmem_limit_bytes=VMEM_LIMIT),
        name="outproj",
    )(x, mods, hx, u, vn, gw, gb, wt, wb)


def _gate_rows(zgc, zg):
    def rows(z):
        bsz, _, t = z.shape
        return z.reshape(bsz, 4, HEADS, t // CHUNK, CHUNK)
    gc = rows(zgc)
    assert gc.shape[3] <= CTX_ROWS
    gc = jnp.pad(gc, ((0, 0), (0, 0), (0, 0), (0, CTX_ROWS - gc.shape[3]), (0, 0)))
    return jnp.concatenate([gc, rows(zg)], axis=3)


def kernel(x, c, ctx, c_ctx, w_ada, b_ada, norm_ffn1, w_ffn1_in, w_ffn1_out, norm_mix, w_in, conv_w, conv_b, b_igate, b_fgate, mlstm_norm, gmlp_norm, gmlp_w, gmlp_b, w_out, norm_ffn2, w_ffn2_in, w_ffn2_out, final_norm):
    bsz, s, d = x.shape
    depth = w_ada.shape[0]
    assert depth == 1, "context-stream update for non-final layers is not implemented"
    l = 0
    qk_w = 2 * HEADS * DK
    mw = HEADS * DV
    c_qk, c_v, c_o = qk_w, qk_w + mw, qk_w + 2 * mw
    c_gate = c_o + 4 * HEADS

    lat_row = lambda b: b
    ctx_row = lambda b: bsz

    rows = -(-(bsz + 1) // 8) * 8
    cond = jnp.zeros((rows, d), F32).at[:bsz].set(c).at[bsz].set(c_ctx)
    mods = _adaln(cond, w_ada[l], b_ada[l]).reshape(rows, N_MOD, d)

    w1i, w1o = w_ffn1_in[l].astype(BF16), w_ffn1_out[l].astype(BF16)
    w2i, w2o = w_ffn2_in[l].astype(BF16), w_ffn2_out[l].astype(BF16)
    wi = w_in[l]
    wqk, wv, wo = (wi[:, a:b].astype(BF16) for a, b in ((0, c_qk), (c_qk, c_v), (c_v, c_o)))
    wg = jnp.pad(wi[:, c_o:c_gate].astype(BF16), ((0, 0), (0, GATE_PAD - 4 * HEADS)))
    wm = wi[:, c_gate:].astype(BF16)
    cw = jnp.pad(conv_w[l], ((0, 8 - CONV_K), (0, 0)))
    cb = conv_b[l].reshape(1, qk_w)
    wout = w_out[l].astype(BF16)

    x = _ffn(x, mods, lat_row, 0, norm_ffn1[l], w1i, w1o)
    ctx = _ffn(ctx.reshape(1, -1, d), mods, ctx_row, 0, norm_ffn1[l], w1i, w1o).reshape(ctx.shape)

    q, kt, v, zg, o, u, vn = _inproj(x, mods, lat_row, norm_mix[l], wqk, wv, wg, cw, cb,
                                     wo, wm, gmlp_norm[l].reshape(1, -1))
    ktc, vc, zgc = _inproj(ctx, mods, ctx_row, norm_mix[l], wqk[:, qk_w // 2:], wv, wg,
                           cw[:, qk_w // 2:], cb[:, qk_w // 2:])

    bias = jnp.stack([b_igate[l][0], b_fgate[l][0], b_igate[l][1], b_fgate[l][1]], axis=-1)
    hx = _mlstm(q, kt, v, o, _gate_rows(zgc, zg), ktc, vc, bias.reshape(HEADS, 4, 1, 1),
                mlstm_norm[l].reshape(1, mw))

    gb = jnp.repeat(gmlp_b[l].T, GD, axis=1)
    x = _outproj(x, mods, hx, u, vn, gmlp_w[l].astype(BF16), gb, wout[:mw], wout[mw:])

    return _ffn(x, mods, lat_row, 6, norm_ffn2[l], w2i, w2o, final_g=final_norm)
```

```python
import functools

import jax
import jax.numpy as jnp
from jax import lax
from jax.experimental import pallas as pl
from jax.experimental.pallas import tpu as pltpu

F32 = jnp.float32
BF16 = jnp.bfloat16

EPS = 1e-6
N_MOD = 9
HEADS = 4
DK = 128
DV = 256
CHUNK = 128
GROUPS = 8
GD = 128
CONV_K = 5
HALO = 16
COL_TILE = 256
LOOKAHEAD = 1
GATE_PAD = 128
GATE_ROWS = 4 * HEADS
ROW_SLAB = 16
CTX_ROWS = 8
NORM_COLS = 128

VMEM_LIMIT = 56 * 1024 * 1024


def _sigmoid(x):
    return 1.0 / (1.0 + jnp.exp(-x))


def _log_sigmoid(x):
    return jnp.minimum(x, 0.0) - jnp.log(1.0 + jnp.exp(-jnp.abs(x)))


def _gelu_tanh(x):
    c2 = -2.0 * 0.7978845608028654
    return x / (1.0 + jnp.exp(x * (c2 + (c2 * 0.044715) * (x * x))))


def _rms(x, g):
    return x * lax.rsqrt(jnp.mean(x * x, axis=-1, keepdims=True) + EPS) * g


def _norm_mod(x, g, shift, scale):
    return _rms(x, g) * (1.0 + scale) + shift


def _dot(a, b):
    return jnp.dot(a, b, preferred_element_type=F32)


def _ada_kernel(c_ref, w_ref, b_ref, o_ref):
    c = c_ref[...]
    s = (c * _sigmoid(c)).astype(BF16)
    o_ref[...] = _dot(s, w_ref[...].astype(BF16)) + b_ref[...]


def _adaln(cond, w, b, tn=1024):
    rows, d = cond.shape
    n = w.shape[1]
    return pl.pallas_call(
        _ada_kernel,
        grid=(n // tn,),
        in_specs=[pl.BlockSpec((rows, d), lambda j: (0, 0)),
                  pl.BlockSpec((d, tn), lambda j: (0, j)),
                  pl.BlockSpec((1, tn), lambda j: (0, j))],
        out_specs=pl.BlockSpec((rows, tn), lambda j: (0, j)),
        out_shape=jax.ShapeDtypeStruct((rows, n), F32),
        compiler_params=pltpu.CompilerParams(dimension_semantics=("parallel",),
                                             vmem_limit_bytes=VMEM_LIMIT),
        name="adaln",
    )(cond, w, b.reshape(1, n))


def _ffn_kernel(*refs, mod_base, final):
    if final:
        x_ref, mod_ref, g_ref, wg_ref, wu_ref, wo_ref, fn_ref, o_ref, xn_ref, r_ref = refs
    else:
        x_ref, mod_ref, g_ref, wg_ref, wu_ref, wo_ref, o_ref, xn_ref = refs
    f = pl.program_id(2)
    tm = xn_ref.shape[0]

    @pl.when(f == 0)
    def _():
        shift = mod_ref[0, mod_base:mod_base + 1, :]
        gain = g_ref[...] * (1.0 + mod_ref[0, mod_base + 1:mod_base + 2, :])

        def slab(i, carry):
            rows = pl.ds(pl.multiple_of(i * ROW_SLAB, ROW_SLAB), ROW_SLAB)
            x = x_ref[0, rows, :]
            r = lax.rsqrt(jnp.mean(x * x, axis=-1, keepdims=True) + EPS)
            xn_ref[rows, :] = (x * r * gain + shift).astype(BF16)
            o_ref[0, rows, :] = jnp.zeros_like(x)
            return carry

        lax.fori_loop(0, tm // ROW_SLAB, slab, 0, unroll=8)

    xn = xn_ref[...]
    gate = _dot(xn, wg_ref[0])
    up = _dot(xn, wu_ref[0])
    h = (gate * _sigmoid(gate) * up).astype(BF16)
    o_ref[0] += _dot(h, wo_ref[...])

    @pl.when(f == pl.num_programs(2) - 1)
    def _():
        half_gate = 0.5 * mod_ref[0, mod_base + 2:mod_base + 3, :]

        def slab(i, carry):
            rows = pl.ds(pl.multiple_of(i * ROW_SLAB, ROW_SLAB), ROW_SLAB)
            y = x_ref[0, rows, :] + half_gate * o_ref[0, rows, :]
            o_ref[0, rows, :] = y
            if final:
                r_ref[rows, :] = lax.rsqrt(jnp.mean(y * y, axis=-1, keepdims=True) + EPS)
            return carry

        lax.fori_loop(0, tm // ROW_SLAB, slab, 0, unroll=8)

        if final:
            def scale(i, carry):
                rows = pl.ds(pl.multiple_of(i * ROW_SLAB, ROW_SLAB), ROW_SLAB)
                o_ref[0, rows, :] = o_ref[0, rows, :] * r_ref[rows, :] * fn_ref[...]
                return carry

            lax.fori_loop(0, tm // ROW_SLAB, scale, 0, unroll=8)


def _ffn_weights(w_in, w_out, tf=512):
    d, ff2 = w_in.shape
    return w_in.astype(BF16).reshape(d, ff2 // tf, tf).transpose(1, 0, 2), w_out.astype(BF16)


def _ffn(x, mods, mod_row, mod_base, g, w_in, w_out, final_g=None, tm=1024):
    bsz, s, d = x.shape
    ff = w_out.shape[0]
    tf = w_in.shape[2]
    tm = min(tm, s)
    nf = ff // tf
    final = final_g is not None
    in_specs = [pl.BlockSpec((1, tm, d), lambda b, i, f: (b, i, 0)),
                pl.BlockSpec((1, N_MOD, d), lambda b, i, f: (mod_row(b), 0, 0)),
                pl.BlockSpec((1, d), lambda b, i, f: (0, 0)),
                pl.BlockSpec((1, d, tf), lambda b, i, f: (f, 0, 0)),
                pl.BlockSpec((1, d, tf), lambda b, i, f: (nf + f, 0, 0)),
                pl.BlockSpec((tf, d), lambda b, i, f: (f, 0))]
    args = [x, mods, g.reshape(1, d), w_in, w_in, w_out]
    if final:
        in_specs.append(pl.BlockSpec((1, d), lambda b, i, f: (0, 0)))
        args.append(final_g.reshape(1, d))
    return pl.pallas_call(
        functools.partial(_ffn_kernel, mod_base=mod_base, final=final),
        grid=(bsz, s // tm, nf),
        in_specs=in_specs,
        out_specs=pl.BlockSpec((1, tm, d), lambda b, i, f: (b, i, 0)),
        out_shape=jax.ShapeDtypeStruct((bsz, s, d), F32),
        scratch_shapes=[pltpu.VMEM((tm, d), BF16)] + ([pltpu.VMEM((tm, 1), F32)] if final else []),
        compiler_params=pltpu.CompilerParams(
            dimension_semantics=("parallel", "parallel", "arbitrary"),
            vmem_limit_bytes=VMEM_LIMIT),
        name="ffn_final" if final else "ffn",
    )(*args)


def _inproj_kernel(*refs, ctx_mode, tm):
    if ctx_mode:
        (x_ref, xp_ref, xq_ref, mod_ref, g_ref, wqk_ref, wv_ref, wg_ref, cw_ref, cb_ref,
         k_ref, v_ref, zg_ref, zs_ref, xn_ref, zgs_ref) = refs
    else:
        (x_ref, xp_ref, xq_ref, mod_ref, g_ref, wqk_ref, wv_ref, wg_ref, cw_ref, cb_ref,
         wo_ref, wm_ref, gn_ref,
         q_ref, k_ref, v_ref, zg_ref, o_ref, u_ref, vn_ref, zs_ref, xn_ref, zgs_ref, gv_ref) = refs
    i = pl.program_id(1)
    g = g_ref[...]
    shift = mod_ref[0, 3:4, :]
    gain = g * (1.0 + mod_ref[0, 4:5, :])

    def norm_rows(x):
        r = lax.rsqrt(jnp.mean(x * x, axis=-1, keepdims=True) + EPS)
        return (x * r * gain + shift).astype(BF16)

    xn_ref[0:HALO, :] = norm_rows(xp_ref[0])
    xn_ref[HALO + tm:, :] = norm_rows(xq_ref[0])

    def slab(t, carry):
        rows = pl.ds(pl.multiple_of(t * ROW_SLAB, ROW_SLAB), ROW_SLAB)
        xn_ref[pl.ds(pl.multiple_of(HALO + t * ROW_SLAB, ROW_SLAB), ROW_SLAB), :] = norm_rows(x_ref[0, rows, :])
        return carry

    lax.fori_loop(0, tm // ROW_SLAB, slab, 0, unroll=8)
    xn = xn_ref[HALO:HALO + tm, :]

    def col(t):
        return slice(t * COL_TILE, (t + 1) * COL_TILE)

    tasks = []
    ssq = [jnp.zeros((tm, 1), F32)]

    if not ctx_mode:
        gw = wm_ref.shape[1] // 2

        def gv_epilogue(z, t):
            gv = _gelu_tanh(z)
            gv_ref[:, col(t)] = gv
            ssq[0] = ssq[0] + jnp.sum(gv * gv, axis=-1, keepdims=True)

        def u_epilogue(z, t):
            u_ref[0, :, col(t)] = _gelu_tanh(z).astype(BF16)

        for t in range(gw // COL_TILE):
            tasks.append((functools.partial(lambda t: _dot(xn, wm_ref[:, gw + t * COL_TILE:gw + (t + 1) * COL_TILE]), t),
                          functools.partial(gv_epilogue, t=t)))
        for t in range(gw // COL_TILE):
            tasks.append((functools.partial(lambda t: _dot(xn, wm_ref[:, col(t)]), t),
                          functools.partial(u_epilogue, t=t)))

    wq = wqk_ref.shape[1]
    k_off = 0 if ctx_mode else wq // 2

    def qk_epilogue(z, t):
        zs = zs_ref.at[t % 2]
        zs[...] = z
        zs[0:HALO, :] = jnp.where(i == 0, 0.0, z[0:HALO])
        zs[HALO + tm:, :] = jnp.where(i == pl.num_programs(1) - 1, 0.0, z[HALO + tm:])
        conv = cb_ref[:, col(t)]
        for j in range(CONV_K):
            off = HALO - CONV_K // 2 + j
            conv = conv + zs[off:off + tm, :] * cw_ref[j:j + 1, col(t)]
        qk = conv * _sigmoid(conv)
        for hh in range(COL_TILE // DK):
            c0 = t * COL_TILE + hh * DK
            if c0 < k_off:
                q_ref[0, c0 // DK] = (qk[:, hh * DK:(hh + 1) * DK] * (DK ** -0.5)).astype(BF16)
            else:
                for cc in range(tm // CHUNK):
                    k_ref[0, (c0 - k_off) // DK, cc] = (
                        qk[cc * CHUNK:(cc + 1) * CHUNK, hh * DK:(hh + 1) * DK].T.astype(BF16))

    for t in range(wq // COL_TILE):
        tasks.append((functools.partial(lambda t: _dot(xn_ref[...], wqk_ref[:, col(t)]), t),
                      functools.partial(qk_epilogue, t=t)))

    if not ctx_mode:
        def o_epilogue(z, h):
            o_ref[0, h] = _sigmoid(z).astype(BF16)

        for h in range(HEADS):
            tasks.append((functools.partial(lambda h: _dot(xn, wo_ref[:, h * DV:(h + 1) * DV]), h),
                          functools.partial(o_epilogue, h=h)))

    def v_epilogue(z, h):
        if h == 0 and not ctx_mode:
            r = lax.rsqrt(ssq[0] * (1.0 / gw) + EPS)
            vn_ref[0] = (gv_ref[...] * r * gn_ref[...]).astype(BF16)
        v_ref[0, h] = z.astype(BF16)

    for h in range(HEADS):
        tasks.append((functools.partial(lambda h: _dot(xn, wv_ref[:, h * DV:(h + 1) * DV]), h),
                      functools.partial(v_epilogue, h=h)))

    def gate_epilogue(z):
        zgs_ref[...] = z
        zg_ref[0] = zgs_ref[...].T[:GATE_ROWS, :]

    tasks.append((lambda: _dot(xn, wg_ref[...]), gate_epilogue))

    pending = [product() for product, _ in tasks[:LOOKAHEAD]]
    for n, (_, epilogue) in enumerate(tasks):
        if n + LOOKAHEAD < len(tasks):
            pending.append(tasks[n + LOOKAHEAD][0]())
        epilogue(pending.pop(0))


def _const_spec(shape):
    nd = len(shape)
    return pl.BlockSpec(shape, lambda b, i: (0,) * nd, pipeline_mode=pl.Buffered(1))


def _inproj(x, mods, mod_row, g, wqk, wv, wg, cw, cb, wo=None, wm=None, gn=None, tm=512):
    bsz, s, d = x.shape
    ctx_mode = wo is None
    tm = min(tm, s)
    nh = s // HALO
    r = tm // HALO
    wq = wqk.shape[1]
    in_specs = [pl.BlockSpec((1, tm, d), lambda b, i: (b, i, 0)),
                pl.BlockSpec((1, HALO, d), lambda b, i: (b, jnp.maximum(i * r - 1, 0), 0)),
                pl.BlockSpec((1, HALO, d), lambda b, i: (b, jnp.minimum((i + 1) * r, nh - 1), 0)),
                pl.BlockSpec((1, N_MOD, d), lambda b, i: (mod_row(b), 0, 0)),
                _const_spec((1, d)), _const_spec(wqk.shape), _const_spec(wv.shape),
                _const_spec(wg.shape), _const_spec(cw.shape), _const_spec(cb.shape)]
    args = [x, x, x, mods, g.reshape(1, d), wqk, wv, wg, cw, cb]

    def tok(width, dtype):
        return (pl.BlockSpec((1, tm, width), lambda b, i: (b, i, 0)),
                jax.ShapeDtypeStruct((bsz, s, width), dtype))

    kt = (pl.BlockSpec((1, HEADS, tm // CHUNK, DK, CHUNK), lambda b, i: (b, 0, i, 0, 0)),
          jax.ShapeDtypeStruct((bsz, HEADS, s // CHUNK, DK, CHUNK), BF16))
    zgt = (pl.BlockSpec((1, GATE_ROWS, tm), lambda b, i: (b, 0, i)),
           jax.ShapeDtypeStruct((bsz, GATE_ROWS, s), F32))
    def heads(width):
        return (pl.BlockSpec((1, HEADS, tm, width), lambda b, i: (b, 0, i, 0)),
                jax.ShapeDtypeStruct((bsz, HEADS, s, width), BF16))

    if ctx_mode:
        outs = [kt, heads(DV), zgt]
    else:
        in_specs += [_const_spec(wo.shape), _const_spec(wm.shape), _const_spec(gn.shape)]
        args += [wo, wm, gn]
        outs = [heads(DK), kt, heads(DV), zgt,
                heads(DV), tok(wm.shape[1] // 2, BF16), tok(wm.shape[1] // 2, BF16)]
    return pl.pallas_call(
        functools.partial(_inproj_kernel, ctx_mode=ctx_mode, tm=tm),
        grid=(bsz, s // tm),
        in_specs=in_specs,
        out_specs=[o[0] for o in outs],
        out_shape=[o[1] for o in outs],
        scratch_shapes=[pltpu.VMEM((2, tm + 2 * HALO, COL_TILE), F32), pltpu.VMEM((tm + 2 * HALO, d), BF16),
                        pltpu.VMEM((tm, GATE_PAD), F32)]
        + ([] if ctx_mode else [pltpu.VMEM((tm, wm.shape[1] // 2), F32)]),
        compiler_params=pltpu.CompilerParams(dimension_semantics=("parallel", "parallel"),
                                             vmem_limit_bytes=VMEM_LIMIT),
        name="inproj_ctx" if ctx_mode else "inproj",
    )(*args)


def _lane_scan(x, op, ident, reverse):
    lane = lax.broadcasted_iota(jnp.int32, x.shape, 1)
    n = x.shape[1]
    k = 1
    while k < n:
        if reverse:
            x = op(x, jnp.where(lane < n - k, pltpu.roll(x, n - k, axis=1), ident))
        else:
            x = op(x, jnp.where(lane >= k, pltpu.roll(x, k, axis=1), ident))
        k *= 2
    return x


def _mlstm_kernel(q_ref, kt_ref, v_ref, o_ref, g_ref, ktc_ref, vc_ref, br_ref, ng_ref, out_ref,
                  e_ref, col_ref, tot_ref, emax_ref, rhs_ref, ms_ref, c_ref, m_ref, lhs_ref, floor_ref):
    s = q_ref.shape[2]
    nc = s // CHUNK
    ncc = vc_ref.shape[2] // CHUNK
    row = lax.broadcasted_iota(jnp.int32, (CHUNK, CHUNK), 0)
    col = lax.broadcasted_iota(jnp.int32, (CHUNK, CHUNK), 1)
    eye = row == col
    ones = jnp.ones((CHUNK, NORM_COLS), BF16)
    sel_r = lax.broadcasted_iota(jnp.int32, (3 * CHUNK, 2 * NORM_COLS), 0)
    sel_c = lax.broadcasted_iota(jnp.int32, (3 * CHUNK, 2 * NORM_COLS), 1)
    sel = jnp.where((sel_r < CHUNK) == (sel_c < NORM_COLS), 1.0, 0.0).astype(BF16)

    for d in range(2):
        logi = g_ref[0, 2 * d, 0] + br_ref[0, 2 * d]
        logf = _log_sigmoid(g_ref[0, 2 * d + 1, 0] + br_ref[0, 2 * d + 1])
        b = _lane_scan(logf, jnp.add, 0.0, d == 1)
        e = logi - b
        e_ref[d] = e
        b_hi = b.astype(BF16).astype(F32)
        col_ref[d, 0] = _lane_scan(e, jnp.maximum, -jnp.inf, d == 1).astype(BF16).astype(F32)
        col_ref[d, 1] = b_hi
        col_ref[d, 2] = (b - b_hi).astype(BF16).astype(F32)
        tot_ref[d] = jnp.sum(logf, axis=-1, keepdims=True)
        emax_ref[d] = jnp.max(e, axis=-1, keepdims=True)

    def scan_step(d, r, kt, v):
        e_r = e_ref[d, pl.ds(r, 1), :]
        emax = emax_ref[d, pl.ds(r, 1), :]
        kw = (kt.astype(F32) * jnp.exp(e_r - emax)).astype(BF16)
        v_aug = jnp.concatenate([v, ones], axis=1)
        upd = _dot(kw, v_aug)
        m = m_ref[d]
        c = c_ref[d]
        mx = jnp.maximum(m, emax)
        c_ref[d] = jnp.exp(m - mx) * c + jnp.exp(emax - mx) * upd
        m_ref[d] = tot_ref[d, pl.ds(r, 1), :] + mx
        return c.astype(BF16), m, v_aug

    c_ref[...] = jnp.zeros_like(c_ref)
    m_ref[...] = jnp.zeros_like(m_ref)
    for t in range(ncc):
        for d in range(2):
            ci = t if d == 0 else ncc - 1 - t
            scan_step(d, ci, ktc_ref[0, 0, ci], vc_ref[0, 0, ci * CHUNK:(ci + 1) * CHUNK, :])

    def scan_body(t, carry):
        for d in range(2):
            ci = t if d == 0 else nc - 1 - t
            rows = pl.ds(pl.multiple_of(ci * CHUNK, CHUNK), CHUNK)
            c_in, m_in, v_aug = scan_step(d, CTX_ROWS + ci, kt_ref[0, 0, ci], v_ref[0, 0, rows, :])
            rhs_ref[d, ci, 0:DK, :] = c_in
            rhs_ref[d, ci, DK:, :] = v_aug
            ms_ref[d, ci] = m_in
        return carry

    lax.fori_loop(0, nc, scan_body, 0, unroll=4)

    def weights(ci, slot):
        q = q_ref[0, 0, pl.ds(pl.multiple_of(ci * CHUNK, CHUNK), CHUNK), :]
        qf = q.astype(F32)
        qk = _dot(q, kt_ref[0, 0, ci])
        r = pl.ds(CTX_ROWS + ci, 1)
        for d in range(2):
            inc = (col <= row) if d == 0 else (col >= row)
            e_m = jnp.where(inc, e_ref[d, r, :], -jnp.inf)
            diag = jnp.concatenate([jnp.where(eye, col_ref[d, i, r, :], 0.0) for i in range(3)], axis=1)
            rep = _dot(diag.astype(BF16), sel)
            m = ms_ref[d, ci]
            mm = jnp.maximum(m, rep[:, :NORM_COLS])
            p = (qk * jnp.exp(e_m - mm)).astype(BF16)
            qa = (qf * jnp.exp(m - mm)).astype(BF16)
            lhs_ref[slot, d] = jnp.concatenate([qa, p], axis=1)
            floor_ref[slot, d] = jnp.exp(-(rep[:, NORM_COLS:] + mm))

    def apply(ci, slot):
        rows = pl.ds(pl.multiple_of(ci * CHUNK, CHUNK), CHUNK)
        h = None
        for d in range(2):
            comb = _dot(lhs_ref[slot, d], rhs_ref[d, ci])
            inv = 1.0 / jnp.maximum(jnp.abs(comb[:, DV:]), floor_ref[slot, d])
            hd = comb[:, :DV] * jnp.concatenate([inv] * (DV // NORM_COLS), axis=1)
            h = hd if h is None else h + hd
        out_ref[0, 0, rows, :] = (_rms(h, ng_ref[...]) * o_ref[0, 0, rows, :].astype(F32)).astype(BF16)

    def out_body(ci, carry):
        apply(ci - 1, (ci - 1) % 2)
        weights(jnp.minimum(ci, nc - 1), ci % 2)
        return carry

    weights(0, 0)
    lax.fori_loop(1, nc + 1, out_body, 0, unroll=4)


def _mlstm(q, kt, v, o, g, ktc, vc, bias, norm_g):
    bsz, _, s, _ = q.shape
    sc = vc.shape[2]
    nc, ncc = s // CHUNK, sc // CHUNK
    nr = g.shape[3]
    return pl.pallas_call(
        _mlstm_kernel,
        grid=(bsz, HEADS),
        in_specs=[pl.BlockSpec((1, 1, s, DK), lambda b, h: (b, h, 0, 0)),
                  pl.BlockSpec((1, 1, nc, DK, CHUNK), lambda b, h: (b, h, 0, 0, 0)),
                  pl.BlockSpec((1, 1, s, DV), lambda b, h: (b, h, 0, 0)),
                  pl.BlockSpec((1, 1, s, DV), lambda b, h: (b, h, 0, 0)),
                  pl.BlockSpec((1, 4, 1, nr, CHUNK), lambda b, h: (b, 0, h, 0, 0)),
                  pl.BlockSpec((1, 1, ncc, DK, CHUNK), lambda b, h: (b, h, 0, 0, 0)),
                  pl.BlockSpec((1, 1, sc, DV), lambda b, h: (b, h, 0, 0)),
                  pl.BlockSpec((1, 4, 1, 1), lambda b, h: (h, 0, 0, 0)),
                  pl.BlockSpec((1, DV), lambda b, h: (0, h))],
        out_specs=pl.BlockSpec((1, 1, s, DV), lambda b, h: (b, h, 0, 0)),
        out_shape=jax.ShapeDtypeStruct((bsz, HEADS, s, DV), BF16),
        scratch_shapes=[pltpu.VMEM((2, nr, CHUNK), F32), pltpu.VMEM((2, 3, nr, CHUNK), F32),
                        pltpu.VMEM((2, nr, 1), F32), pltpu.VMEM((2, nr, 1), F32),
                        pltpu.VMEM((2, nc, DK + CHUNK, DV + NORM_COLS), BF16), pltpu.VMEM((2, nc, 1, 1), F32),
                        pltpu.VMEM((2, DK, DV + NORM_COLS), F32), pltpu.VMEM((2, 1, 1), F32),
                        pltpu.VMEM((2, 2, CHUNK, DK + CHUNK), BF16), pltpu.VMEM((2, 2, CHUNK, NORM_COLS), F32)],
        compiler_params=pltpu.CompilerParams(dimension_semantics=("parallel", "parallel"),
                                             vmem_limit_bytes=VMEM_LIMIT),
        name="mlstm",
    )(q, kt, v, o, g, ktc, vc, bias, norm_g)


def _outproj_kernel(x_ref, mod_ref, hx_ref, u_ref, vn_ref, gw_ref, gb_ref, wt_ref, wb_ref, o_ref, gx_ref):
    tm = x_ref.shape[1]
    for ci in range(tm // CHUNK):
        rows = slice(ci * CHUNK, (ci + 1) * CHUNK)
        for g in range(GROUPS):
            cols = slice(g * GD, (g + 1) * GD)
            sp = _dot(gw_ref[g], vn_ref[0, rows, cols]) + gb_ref[:, cols]
            gx_ref[rows, cols] = (u_ref[0, rows, cols].astype(F32) * sp).astype(BF16)
    y = _dot(gx_ref[...], wb_ref[...])
    for h in range(HEADS):
        y = y + _dot(hx_ref[0, h], wt_ref[h * DV:(h + 1) * DV, :])
    o_ref[0] = x_ref[0] + mod_ref[0, 5:6, :] * y


def _outproj(x, mods, hx, u, vn, gw, gb, wt, wb, tm=512):
    bsz, s, d = x.shape
    w = u.shape[2]
    return pl.pallas_call(
        _outproj_kernel,
        grid=(bsz, s // tm),
        in_specs=[pl.BlockSpec((1, tm, d), lambda b, i: (b, i, 0)),
                  pl.BlockSpec((1, N_MOD, d), lambda b, i: (b, 0, 0)),
                  pl.BlockSpec((1, HEADS, tm, DV), lambda b, i: (b, 0, i, 0)),
                  pl.BlockSpec((1, tm, w), lambda b, i: (b, i, 0)),
                  pl.BlockSpec((1, tm, w), lambda b, i: (b, i, 0)),
                  _const_spec(gw.shape), _const_spec(gb.shape),
                  _const_spec(wt.shape), _const_spec(wb.shape)],
        out_specs=pl.BlockSpec((1, tm, d), lambda b, i: (b, i, 0)),
        out_shape=jax.ShapeDtypeStruct((bsz, s, d), F32),
        scratch_shapes=[pltpu.VMEM((tm, w), BF16)],
        compiler_params=pltpu.CompilerParams(dimension_semantics=("parallel", "parallel"),
                                             vmem_limit_bytes=VMEM_LIMIT),
        name="outproj",
    )(x, mods, hx, u, vn, gw, gb, wt, wb)


def _gate_rows(zgc, zg):
    def rows(z):
        bsz, _, t = z.shape
        return z.reshape(bsz, 4, HEADS, t // CHUNK, CHUNK)
    gc = rows(zgc)
    assert gc.shape[3] <= CTX_ROWS
    gc = jnp.pad(gc, ((0, 0), (0, 0), (0, 0), (0, CTX_ROWS - gc.shape[3]), (0, 0)))
    return jnp.concatenate([gc, rows(zg)], axis=3)


def kernel(x, c, ctx, c_ctx, w_ada, b_ada, norm_ffn1, w_ffn1_in, w_ffn1_out, norm_mix, w_in, conv_w, conv_b, b_igate, b_fgate, mlstm_norm, gmlp_norm, gmlp_w, gmlp_b, w_out, norm_ffn2, w_ffn2_in, w_ffn2_out, final_norm):
    bsz, s, d = x.shape
    depth = w_ada.shape[0]
    assert depth == 1, "context-stream update for non-final layers is not implemented"
    l = 0
    qk_w = 2 * HEADS * DK
    mw = HEADS * DV
    c_qk, c_v, c_o = qk_w, qk_w + mw, qk_w + 2 * mw
    c_gate = c_o + 4 * HEADS

    lat_row = lambda b: b
    ctx_row = lambda b: bsz

    rows = -(-(bsz + 1) // 8) * 8
    cond = jnp.zeros((rows, d), F32).at[:bsz].set(c).at[bsz].set(c_ctx)
    mods = _adaln(cond, w_ada[l], b_ada[l]).reshape(rows, N_MOD, d)

    w1i, w1o = _ffn_weights(w_ffn1_in[l], w_ffn1_out[l])
    w2i, w2o = _ffn_weights(w_ffn2_in[l], w_ffn2_out[l])
    wi = w_in[l]
    wqk, wv, wo = (wi[:, a:b].astype(BF16) for a, b in ((0, c_qk), (c_qk, c_v), (c_v, c_o)))
    wg = jnp.pad(wi[:, c_o:c_gate].astype(BF16), ((0, 0), (0, GATE_PAD - 4 * HEADS)))
    wm = wi[:, c_gate:].astype(BF16)
    cw = jnp.pad(conv_w[l], ((0, 8 - CONV_K), (0, 0)))
    cb = conv_b[l].reshape(1, qk_w)
    wout = w_out[l].astype(BF16)

    x = _ffn(x, mods, lat_row, 0, norm_ffn1[l], w1i, w1o)
    ctx = _ffn(ctx.reshape(1, -1, d), mods, ctx_row, 0, norm_ffn1[l], w1i, w1o).reshape(ctx.shape)

    q, kt, v, zg, o, u, vn = _inproj(x, mods, lat_row, norm_mix[l], wqk, wv, wg, cw, cb,
                                     wo, wm, gmlp_norm[l].reshape(1, -1))
    ktc, vc, zgc = _inproj(ctx, mods, ctx_row, norm_mix[l], wqk[:, qk_w // 2:], wv, wg,
                           cw[:, qk_w // 2:], cb[:, qk_w // 2:])

    bias = jnp.stack([b_igate[l][0], b_fgate[l][0], b_igate[l][1], b_fgate[l][1]], axis=-1)
    hx = _mlstm(q, kt, v, o, _gate_rows(zgc, zg), ktc, vc, bias.reshape(HEADS, 4, 1, 1),
                mlstm_norm[l].reshape(1, mw))

    gb = jnp.repeat(gmlp_b[l].T, GD, axis=1)
    x = _outproj(x, mods, hx, u, vn, gmlp_w[l].astype(BF16), gb, wout[:mw], wout[mw:])

    return _ffn(x, mods, lat_row, 6, norm_ffn2[l], w2i, w2o, final_g=final_norm)
```

```python
import functools

import jax
import jax.numpy as jnp
from jax import lax
from jax.experimental import pallas as pl
from jax.experimental.pallas import tpu as pltpu

F32 = jnp.float32
BF16 = jnp.bfloat16

EPS = 1e-6
N_MOD = 9
HEADS = 4
DK = 128
DV = 256
CHUNK = 128
GROUPS = 8
GD = 128
CONV_K = 5
HALO = 16
COL_TILE = 256
LOOKAHEAD = 1
LANE = 128
SIDE_ROWS = 1024
SIDE_COLS = LANE
GATE_PAD = LANE
GATE_ROWS = 4 * HEADS
ROW_SLAB = 16
CTX_ROWS = 8
NORM_COLS = LANE

VMEM_LIMIT = 56 * 1024 * 1024


def _sigmoid(x):
    return 1.0 / (1.0 + jnp.exp(-x))


def _log_sigmoid(x):
    return jnp.minimum(x, 0.0) - jnp.log(1.0 + jnp.exp(-jnp.abs(x)))


def _gelu_tanh(x):
    c2 = -2.0 * 0.7978845608028654
    return x / (1.0 + jnp.exp(x * (c2 + (c2 * 0.044715) * (x * x))))


def _rms(x, g):
    return x * lax.rsqrt(jnp.mean(x * x, axis=-1, keepdims=True) + EPS) * g


def _norm_mod(x, g, shift, scale):
    return _rms(x, g) * (1.0 + scale) + shift


def _dot(a, b):
    return jnp.dot(a, b, preferred_element_type=F32)


def _ada_kernel(c_ref, w_ref, b_ref, o_ref):
    c = c_ref[...]
    s = (c * _sigmoid(c)).astype(BF16)
    o_ref[...] = _dot(s, w_ref[...].astype(BF16)) + b_ref[...]


def _adaln(cond, w, b, tn=1024):
    rows, d = cond.shape
    n = w.shape[1]
    return pl.pallas_call(
        _ada_kernel,
        grid=(n // tn,),
        in_specs=[pl.BlockSpec((rows, d), lambda j: (0, 0)),
                  pl.BlockSpec((d, tn), lambda j: (0, j)),
                  pl.BlockSpec((1, tn), lambda j: (0, j))],
        out_specs=pl.BlockSpec((rows, tn), lambda j: (0, j)),
        out_shape=jax.ShapeDtypeStruct((rows, n), F32),
        compiler_params=pltpu.CompilerParams(dimension_semantics=("parallel",),
                                             vmem_limit_bytes=VMEM_LIMIT),
        name="adaln",
    )(cond, w, b.reshape(1, n))


def _ffn_kernel(*refs, mod_base, final, side):
    refs = list(refs)
    x_ref, mod_ref, g_ref, wg_ref, wu_ref, wo_ref = refs[:6]
    del refs[:6]
    fn_ref = refs.pop(0) if final else None
    side_ref = refs.pop(0) if side else None
    o_ref = refs.pop(0)
    side_bf_ref = refs.pop(0) if side else None
    xn_ref = refs.pop(0)
    r_ref = refs.pop(0) if final else None
    f = pl.program_id(2)
    tm = xn_ref.shape[0]

    if side:
        side_bf_ref[...] = side_ref[...].astype(BF16)

    @pl.when(f == 0)
    def _():
        shift = mod_ref[0, mod_base:mod_base + 1, :]
        gain = g_ref[...] * (1.0 + mod_ref[0, mod_base + 1:mod_base + 2, :])

        def slab(i, carry):
            rows = pl.ds(pl.multiple_of(i * ROW_SLAB, ROW_SLAB), ROW_SLAB)
            x = x_ref[0, rows, :]
            r = lax.rsqrt(jnp.mean(x * x, axis=-1, keepdims=True) + EPS)
            xn_ref[rows, :] = (x * r * gain + shift).astype(BF16)
            o_ref[0, rows, :] = jnp.zeros_like(x)
            return carry

        lax.fori_loop(0, tm // ROW_SLAB, slab, 0, unroll=8)

    xn = xn_ref[...]
    gate = _dot(xn, wg_ref[...])
    up = _dot(xn, wu_ref[...])
    h = (gate * _sigmoid(gate) * up).astype(BF16)
    o_ref[0] += _dot(h, wo_ref[...])

    @pl.when(f == pl.num_programs(2) - 1)
    def _():
        half_gate = 0.5 * mod_ref[0, mod_base + 2:mod_base + 3, :]

        def slab(i, carry):
            rows = pl.ds(pl.multiple_of(i * ROW_SLAB, ROW_SLAB), ROW_SLAB)
            y = x_ref[0, rows, :] + half_gate * o_ref[0, rows, :]
            o_ref[0, rows, :] = y
            if final:
                r_ref[rows, :] = lax.rsqrt(jnp.mean(y * y, axis=-1, keepdims=True) + EPS)
            return carry

        lax.fori_loop(0, tm // ROW_SLAB, slab, 0, unroll=8)

        if final:
            def scale(i, carry):
                rows = pl.ds(pl.multiple_of(i * ROW_SLAB, ROW_SLAB), ROW_SLAB)
                o_ref[0, rows, :] = o_ref[0, rows, :] * r_ref[rows, :] * fn_ref[...]
                return carry

            lax.fori_loop(0, tm // ROW_SLAB, scale, 0, unroll=8)


def _ffn(x, mods, mod_row, mod_base, g, w_in, w_out, final_g=None, side=None, tm=1024, tf=512):
    bsz, s, d = x.shape
    ff = w_out.shape[0]
    tm = min(tm, s)
    nf = ff // tf
    final = final_g is not None
    in_specs = [pl.BlockSpec((1, tm, d), lambda b, i, f: (b, i, 0)),
                pl.BlockSpec((1, N_MOD, d), lambda b, i, f: (mod_row(b), 0, 0)),
                pl.BlockSpec((1, d), lambda b, i, f: (0, 0)),
                pl.BlockSpec((d, tf), lambda b, i, f: (0, f)),
                pl.BlockSpec((d, tf), lambda b, i, f: (0, nf + f)),
                pl.BlockSpec((tf, d), lambda b, i, f: (f, 0))]
    args = [x, mods, g.reshape(1, d), w_in, w_in, w_out]
    if final:
        in_specs.append(pl.BlockSpec((1, d), lambda b, i, f: (0, 0)))
        args.append(final_g.reshape(1, d))
    out_specs = [pl.BlockSpec((1, tm, d), lambda b, i, f: (b, i, 0))]
    out_shape = [jax.ShapeDtypeStruct((bsz, s, d), F32)]
    if side is not None:
        nrb = side.shape[0] // SIDE_ROWS
        nblk = nrb * pl.cdiv(side.shape[1], SIDE_COLS)
        assert nrb * SIDE_ROWS == side.shape[0] and nblk <= bsz * (s // tm) * nf

        def side_map(b, i, f):
            k = jnp.minimum((b * (s // tm) + i) * nf + f, nblk - 1)
            return (k % nrb, k // nrb)

        side_spec = pl.BlockSpec((SIDE_ROWS, SIDE_COLS), side_map)
        in_specs.append(side_spec)
        args.append(side)
        out_specs.append(side_spec)
        out_shape.append(jax.ShapeDtypeStruct(side.shape, BF16))
    res = pl.pallas_call(
        functools.partial(_ffn_kernel, mod_base=mod_base, final=final, side=side is not None),
        grid=(bsz, s // tm, nf),
        in_specs=in_specs,
        out_specs=out_specs,
        out_shape=out_shape,
        scratch_shapes=[pltpu.VMEM((tm, d), BF16)] + ([pltpu.VMEM((tm, 1), F32)] if final else []),
        compiler_params=pltpu.CompilerParams(
            dimension_semantics=("arbitrary", "arbitrary", "arbitrary"),
            vmem_limit_bytes=VMEM_LIMIT),
        name="ffn_final" if final else "ffn",
    )(*args)
    return res if side is not None else res[0]


def _inproj_kernel(*refs, ctx_mode, tm):
    if ctx_mode:
        (x_ref, xp_ref, xq_ref, mod_ref, g_ref, wqk_ref, wv_ref, wg_ref, cw_ref, cb_ref,
         k_ref, v_ref, zg_ref, zs_ref, xn_ref, zgs_ref) = refs
    else:
        (x_ref, xp_ref, xq_ref, mod_ref, g_ref, wqk_ref, wv_ref, wg_ref, cw_ref, cb_ref,
         wo_ref, wm_ref, gn_ref,
         q_ref, k_ref, v_ref, zg_ref, o_ref, u_ref, vn_ref, zs_ref, xn_ref, zgs_ref, gv_ref) = refs
    i = pl.program_id(1)
    g = g_ref[...]
    shift = mod_ref[0, 3:4, :]
    gain = g * (1.0 + mod_ref[0, 4:5, :])

    def norm_rows(x):
        r = lax.rsqrt(jnp.mean(x * x, axis=-1, keepdims=True) + EPS)
        return (x * r * gain + shift).astype(BF16)

    xn_ref[0:HALO, :] = norm_rows(xp_ref[0])
    xn_ref[HALO + tm:, :] = norm_rows(xq_ref[0])

    def slab(t, carry):
        rows = pl.ds(pl.multiple_of(t * ROW_SLAB, ROW_SLAB), ROW_SLAB)
        xn_ref[pl.ds(pl.multiple_of(HALO + t * ROW_SLAB, ROW_SLAB), ROW_SLAB), :] = norm_rows(x_ref[0, rows, :])
        return carry

    lax.fori_loop(0, tm // ROW_SLAB, slab, 0, unroll=8)
    xn = xn_ref[HALO:HALO + tm, :]

    def col(t):
        return slice(t * COL_TILE, (t + 1) * COL_TILE)

    tasks = []
    ssq = [jnp.zeros((tm, 1), F32)]

    if not ctx_mode:
        gw = wm_ref.shape[1] // 2

        def gv_epilogue(z, t):
            gv = _gelu_tanh(z)
            gv_ref[:, col(t)] = gv
            ssq[0] = ssq[0] + jnp.sum(gv * gv, axis=-1, keepdims=True)

        def u_epilogue(z, t):
            u_ref[0, :, col(t)] = _gelu_tanh(z).astype(BF16)

        for t in range(gw // COL_TILE):
            tasks.append((functools.partial(lambda t: _dot(xn, wm_ref[:, gw + t * COL_TILE:gw + (t + 1) * COL_TILE]), t),
                          functools.partial(gv_epilogue, t=t)))
        for t in range(gw // COL_TILE):
            tasks.append((functools.partial(lambda t: _dot(xn, wm_ref[:, col(t)]), t),
                          functools.partial(u_epilogue, t=t)))

    wq = wqk_ref.shape[1]
    k_off = 0 if ctx_mode else wq // 2

    def qk_epilogue(z, t):
        zs = zs_ref.at[t % 2]
        zs[...] = z
        zs[0:HALO, :] = jnp.where(i == 0, 0.0, z[0:HALO])
        zs[HALO + tm:, :] = jnp.where(i == pl.num_programs(1) - 1, 0.0, z[HALO + tm:])
        conv = cb_ref[:, col(t)]
        for j in range(CONV_K):
            off = HALO - CONV_K // 2 + j
            conv = conv + zs[off:off + tm, :] * cw_ref[j:j + 1, col(t)]
        qk = conv * _sigmoid(conv)
        for hh in range(COL_TILE // DK):
            c0 = t * COL_TILE + hh * DK
            if c0 < k_off:
                q_ref[0, c0 // DK] = (qk[:, hh * DK:(hh + 1) * DK] * (DK ** -0.5)).astype(BF16)
            else:
                for cc in range(tm // CHUNK):
                    k_ref[0, (c0 - k_off) // DK, cc] = (
                        qk[cc * CHUNK:(cc + 1) * CHUNK, hh * DK:(hh + 1) * DK].T.astype(BF16))

    for t in range(wq // COL_TILE):
        tasks.append((functools.partial(lambda t: _dot(xn_ref[...], wqk_ref[:, col(t)]), t),
                      functools.partial(qk_epilogue, t=t)))

    if not ctx_mode:
        def o_epilogue(z, h):
            o_ref[0, h] = _sigmoid(z).astype(BF16)

        for h in range(HEADS):
            tasks.append((functools.partial(lambda h: _dot(xn, wo_ref[:, h * DV:(h + 1) * DV]), h),
                          functools.partial(o_epilogue, h=h)))

    def v_epilogue(z, h):
        if h == 0 and not ctx_mode:
            r = lax.rsqrt(ssq[0] * (1.0 / gw) + EPS)
            vn_ref[0] = (gv_ref[...] * r * gn_ref[...]).astype(BF16)
        v_ref[0, h] = z.astype(BF16)

    for h in range(HEADS):
        tasks.append((functools.partial(lambda h: _dot(xn, wv_ref[:, h * DV:(h + 1) * DV]), h),
                      functools.partial(v_epilogue, h=h)))

    def gate_epilogue(z):
        zgs_ref[...] = z
        zg_ref[0] = zgs_ref[...].T[:GATE_ROWS, :]

    tasks.append((lambda: _dot(xn, wg_ref[...]), gate_epilogue))

    pending = [product() for product, _ in tasks[:LOOKAHEAD]]
    for n, (_, epilogue) in enumerate(tasks):
        if n + LOOKAHEAD < len(tasks):
            pending.append(tasks[n + LOOKAHEAD][0]())
        epilogue(pending.pop(0))


def _const_spec(shape):
    nd = len(shape)
    return pl.BlockSpec(shape, lambda b, i: (0,) * nd, pipeline_mode=pl.Buffered(1))


def _inproj(x, mods, mod_row, g, wi, cw, cb, wm=None, gn=None, tm=512):
    bsz, s, d = x.shape
    ctx_mode = wm is None
    tm = min(tm, s)
    nh = s // HALO
    r = tm // HALO
    qk_w, mw = 2 * HEADS * DK, HEADS * DV
    assert qk_w == mw and (qk_w + 2 * mw) % GATE_PAD == 0
    wq = qk_w // 2 if ctx_mode else qk_w

    def wcols(width, block):
        return pl.BlockSpec((d, width), lambda b, i: (0, block), pipeline_mode=pl.Buffered(1))

    in_specs = [pl.BlockSpec((1, tm, d), lambda b, i: (b, i, 0)),
                pl.BlockSpec((1, HALO, d), lambda b, i: (b, jnp.maximum(i * r - 1, 0), 0)),
                pl.BlockSpec((1, HALO, d), lambda b, i: (b, jnp.minimum((i + 1) * r, nh - 1), 0)),
                pl.BlockSpec((1, N_MOD, d), lambda b, i: (mod_row(b), 0, 0)),
                _const_spec((1, d)),
                wcols(wq, 1 if ctx_mode else 0),
                wcols(mw, 1),
                wcols(GATE_PAD, (qk_w + 2 * mw) // GATE_PAD),
                _const_spec(cw.shape), _const_spec(cb.shape)]
    args = [x, x, x, mods, g.reshape(1, d), wi, wi, wi, cw, cb]

    def tok(width, dtype):
        return (pl.BlockSpec((1, tm, width), lambda b, i: (b, i, 0)),
                jax.ShapeDtypeStruct((bsz, s, width), dtype))

    kt = (pl.BlockSpec((1, HEADS, tm // CHUNK, DK, CHUNK), lambda b, i: (b, 0, i, 0, 0)),
          jax.ShapeDtypeStruct((bsz, HEADS, s // CHUNK, DK, CHUNK), BF16))
    zgt = (pl.BlockSpec((1, GATE_ROWS, tm), lambda b, i: (b, 0, i)),
           jax.ShapeDtypeStruct((bsz, GATE_ROWS, s), F32))
    def heads(width):
        return (pl.BlockSpec((1, HEADS, tm, width), lambda b, i: (b, 0, i, 0)),
                jax.ShapeDtypeStruct((bsz, HEADS, s, width), BF16))

    if ctx_mode:
        outs = [kt, heads(DV), zgt]
    else:
        in_specs += [wcols(mw, 2), _const_spec(wm.shape), _const_spec(gn.shape)]
        args += [wi, wm, gn]
        outs = [heads(DK), kt, heads(DV), zgt,
                heads(DV), tok(wm.shape[1] // 2, BF16), tok(wm.shape[1] // 2, BF16)]
    return pl.pallas_call(
        functools.partial(_inproj_kernel, ctx_mode=ctx_mode, tm=tm),
        grid=(bsz, s // tm),
        in_specs=in_specs,
        out_specs=[o[0] for o in outs],
        out_shape=[o[1] for o in outs],
        scratch_shapes=[pltpu.VMEM((2, tm + 2 * HALO, COL_TILE), F32), pltpu.VMEM((tm + 2 * HALO, d), BF16),
                        pltpu.VMEM((tm, GATE_PAD), F32)]
        + ([] if ctx_mode else [pltpu.VMEM((tm, wm.shape[1] // 2), F32)]),
        compiler_params=pltpu.CompilerParams(dimension_semantics=("parallel", "parallel"),
                                             vmem_limit_bytes=VMEM_LIMIT),
        name="inproj_ctx" if ctx_mode else "inproj",
    )(*args)


def _lane_scan(x, op, ident, reverse):
    lane = lax.broadcasted_iota(jnp.int32, x.shape, 1)
    n = x.shape[1]
    k = 1
    while k < n:
        if reverse:
            x = op(x, jnp.where(lane < n - k, pltpu.roll(x, n - k, axis=1), ident))
        else:
            x = op(x, jnp.where(lane >= k, pltpu.roll(x, k, axis=1), ident))
        k *= 2
    return x


def _mlstm_kernel(q_ref, kt_ref, v_ref, o_ref, g_ref, ktc_ref, vc_ref, br_ref, ng_ref, out_ref,
                  e_ref, col_ref, tot_ref, emax_ref, rhs_ref, ms_ref, c_ref, m_ref, lhs_ref, floor_ref):
    s = q_ref.shape[2]
    nc = s // CHUNK
    ncc = vc_ref.shape[2] // CHUNK
    row = lax.broadcasted_iota(jnp.int32, (CHUNK, CHUNK), 0)
    col = lax.broadcasted_iota(jnp.int32, (CHUNK, CHUNK), 1)
    eye = row == col
    ones = jnp.ones((CHUNK, NORM_COLS), BF16)
    sel_r = lax.broadcasted_iota(jnp.int32, (3 * CHUNK, 2 * NORM_COLS), 0)
    sel_c = lax.broadcasted_iota(jnp.int32, (3 * CHUNK, 2 * NORM_COLS), 1)
    sel = jnp.where((sel_r < CHUNK) == (sel_c < NORM_COLS), 1.0, 0.0).astype(BF16)

    for d in range(2):
        logi = g_ref[0, 2 * d, 0] + br_ref[0, 2 * d]
        logf = _log_sigmoid(g_ref[0, 2 * d + 1, 0] + br_ref[0, 2 * d + 1])
        b = _lane_scan(logf, jnp.add, 0.0, d == 1)
        e = logi - b
        e_ref[d] = e
        b_hi = b.astype(BF16).astype(F32)
        col_ref[d, 0] = _lane_scan(e, jnp.maximum, -jnp.inf, d == 1).astype(BF16).astype(F32)
        col_ref[d, 1] = b_hi
        col_ref[d, 2] = (b - b_hi).astype(BF16).astype(F32)
        tot_ref[d] = jnp.sum(logf, axis=-1, keepdims=True)
        emax_ref[d] = jnp.max(e, axis=-1, keepdims=True)

    def scan_step(d, r, kt, v):
        e_r = e_ref[d, pl.ds(r, 1), :]
        emax = emax_ref[d, pl.ds(r, 1), :]
        kw = (kt.astype(F32) * jnp.exp(e_r - emax)).astype(BF16)
        v_aug = jnp.concatenate([v, ones], axis=1)
        upd = _dot(kw, v_aug)
        m = m_ref[d]
        c = c_ref[d]
        mx = jnp.maximum(m, emax)
        c_ref[d] = jnp.exp(m - mx) * c + jnp.exp(emax - mx) * upd
        m_ref[d] = tot_ref[d, pl.ds(r, 1), :] + mx
        return c.astype(BF16), m, v_aug

    c_ref[...] = jnp.zeros_like(c_ref)
    m_ref[...] = jnp.zeros_like(m_ref)
    for t in range(ncc):
        for d in range(2):
            ci = t if d == 0 else ncc - 1 - t
            scan_step(d, ci, ktc_ref[0, 0, ci], vc_ref[0, 0, ci * CHUNK:(ci + 1) * CHUNK, :])

    def scan_body(t, carry):
        for d in range(2):
            ci = t if d == 0 else nc - 1 - t
            rows = pl.ds(pl.multiple_of(ci * CHUNK, CHUNK), CHUNK)
            c_in, m_in, v_aug = scan_step(d, CTX_ROWS + ci, kt_ref[0, 0, ci], v_ref[0, 0, rows, :])
            rhs_ref[d, ci, 0:DK, :] = c_in
            rhs_ref[d, ci, DK:, :] = v_aug
            ms_ref[d, ci] = m_in
        return carry

    lax.fori_loop(0, nc, scan_body, 0, unroll=4)

    def weights(ci, slot):
        q = q_ref[0, 0, pl.ds(pl.multiple_of(ci * CHUNK, CHUNK), CHUNK), :]
        qf = q.astype(F32)
        qk = _dot(q, kt_ref[0, 0, ci])
        r = pl.ds(CTX_ROWS + ci, 1)
        for d in range(2):
            inc = (col <= row) if d == 0 else (col >= row)
            e_m = jnp.where(inc, e_ref[d, r, :], -jnp.inf)
            diag = jnp.concatenate([jnp.where(eye, col_ref[d, i, r, :], 0.0) for i in range(3)], axis=1)
            rep = _dot(diag.astype(BF16), sel)
            m = ms_ref[d, ci]
            mm = jnp.maximum(m, rep[:, :NORM_COLS])
            p = (qk * jnp.exp(e_m - mm)).astype(BF16)
            qa = (qf * jnp.exp(m - mm)).astype(BF16)
            lhs_ref[slot, d] = jnp.concatenate([qa, p], axis=1)
            floor_ref[slot, d] = jnp.exp(-(rep[:, NORM_COLS:] + mm))

    def apply(ci, slot):
        rows = pl.ds(pl.multiple_of(ci * CHUNK, CHUNK), CHUNK)
        h = None
        for d in range(2):
            comb = _dot(lhs_ref[slot, d], rhs_ref[d, ci])
            inv = 1.0 / jnp.maximum(jnp.abs(comb[:, DV:]), floor_ref[slot, d])
            hd = comb[:, :DV] * jnp.concatenate([inv] * (DV // NORM_COLS), axis=1)
            h = hd if h is None else h + hd
        out_ref[0, 0, rows, :] = (_rms(h, ng_ref[...]) * o_ref[0, 0, rows, :].astype(F32)).astype(BF16)

    def out_body(ci, carry):
        apply(ci - 1, (ci - 1) % 2)
        weights(jnp.minimum(ci, nc - 1), ci % 2)
        return carry

    weights(0, 0)
    lax.fori_loop(1, nc + 1, out_body, 0, unroll=4)


def _mlstm(q, kt, v, o, g, ktc, vc, bias, norm_g):
    bsz, _, s, _ = q.shape
    sc = vc.shape[2]
    nc, ncc = s // CHUNK, sc // CHUNK
    nr = g.shape[3]
    return pl.pallas_call(
        _mlstm_kernel,
        grid=(bsz, HEADS),
        in_specs=[pl.BlockSpec((1, 1, s, DK), lambda b, h: (b, h, 0, 0)),
                  pl.BlockSpec((1, 1, nc, DK, CHUNK), lambda b, h: (b, h, 0, 0, 0)),
                  pl.BlockSpec((1, 1, s, DV), lambda b, h: (b, h, 0, 0)),
                  pl.BlockSpec((1, 1, s, DV), lambda b, h: (b, h, 0, 0)),
                  pl.BlockSpec((1, 4, 1, nr, CHUNK), lambda b, h: (b, 0, h, 0, 0)),
                  pl.BlockSpec((1, 1, ncc, DK, CHUNK), lambda b, h: (b, h, 0, 0, 0)),
                  pl.BlockSpec((1, 1, sc, DV), lambda b, h: (b, h, 0, 0)),
                  pl.BlockSpec((1, 4, 1, 1), lambda b, h: (h, 0, 0, 0)),
                  pl.BlockSpec((1, DV), lambda b, h: (0, h))],
        out_specs=pl.BlockSpec((1, 1, s, DV), lambda b, h: (b, h, 0, 0)),
        out_shape=jax.ShapeDtypeStruct((bsz, HEADS, s, DV), BF16),
        scratch_shapes=[pltpu.VMEM((2, nr, CHUNK), F32), pltpu.VMEM((2, 3, nr, CHUNK), F32),
                        pltpu.VMEM((2, nr, 1), F32), pltpu.VMEM((2, nr, 1), F32),
                        pltpu.VMEM((2, nc, DK + CHUNK, DV + NORM_COLS), BF16), pltpu.VMEM((2, nc, 1, 1), F32),
                        pltpu.VMEM((2, DK, DV + NORM_COLS), F32), pltpu.VMEM((2, 1, 1), F32),
                        pltpu.VMEM((2, 2, CHUNK, DK + CHUNK), BF16), pltpu.VMEM((2, 2, CHUNK, NORM_COLS), F32)],
        compiler_params=pltpu.CompilerParams(dimension_semantics=("parallel", "parallel"),
                                             vmem_limit_bytes=VMEM_LIMIT),
        name="mlstm",
    )(q, kt, v, o, g, ktc, vc, bias, norm_g)


def _outproj_kernel(x_ref, mod_ref, hx_ref, u_ref, vn_ref, gw_ref, gb_ref, wt_ref, wb_ref, w2i_ref, w2o_ref,
                    o_ref, w2i_bf_ref, w2o_bf_ref, gx_ref):
    tm = x_ref.shape[1]
    w2i_bf_ref[...] = w2i_ref[...].astype(BF16)
    w2o_bf_ref[...] = w2o_ref[...].astype(BF16)
    for ci in range(tm // CHUNK):
        rows = slice(ci * CHUNK, (ci + 1) * CHUNK)
        for g in range(GROUPS):
            cols = slice(g * GD, (g + 1) * GD)
            sp = _dot(gw_ref[g], vn_ref[0, rows, cols]) + gb_ref[:, cols]
            gx_ref[rows, cols] = (u_ref[0, rows, cols].astype(F32) * sp).astype(BF16)
    y = _dot(gx_ref[...], wb_ref[...])
    for h in range(HEADS):
        y = y + _dot(hx_ref[0, h], wt_ref[h * DV:(h + 1) * DV, :])
    o_ref[0] = x_ref[0] + mod_ref[0, 5:6, :] * y


def _cast_block(extent, max_blocks):
    for blk in range(LANE, extent + 1, LANE):
        if extent % blk == 0 and extent // blk <= max_blocks:
            return blk
    raise ValueError(f"no block of {extent} fits {max_blocks} grid steps")


def _outproj(x, mods, hx, u, vn, gw, gb, wt, wb, w2_in, w2_out, tm=512):
    bsz, s, d = x.shape
    w = u.shape[2]
    steps = bsz * (s // tm)
    cast_cols = _cast_block(w2_in.shape[1], steps)
    cast_rows = _cast_block(w2_out.shape[0], steps)
    ci, ro = w2_in.shape[1] // cast_cols, w2_out.shape[0] // cast_rows
    i_spec = pl.BlockSpec((w2_in.shape[0], cast_cols), lambda b, i: (0, jnp.minimum(b * (s // tm) + i, ci - 1)))
    o_spec = pl.BlockSpec((cast_rows, w2_out.shape[1]), lambda b, i: (jnp.minimum(b * (s // tm) + i, ro - 1), 0))
    return pl.pallas_call(
        _outproj_kernel,
        grid=(bsz, s // tm),
        in_specs=[pl.BlockSpec((1, tm, d), lambda b, i: (b, i, 0)),
                  pl.BlockSpec((1, N_MOD, d), lambda b, i: (b, 0, 0)),
                  pl.BlockSpec((1, HEADS, tm, DV), lambda b, i: (b, 0, i, 0)),
                  pl.BlockSpec((1, tm, w), lambda b, i: (b, i, 0)),
                  pl.BlockSpec((1, tm, w), lambda b, i: (b, i, 0)),
                  _const_spec(gw.shape), _const_spec(gb.shape),
                  _const_spec(wt.shape), _const_spec(wb.shape), i_spec, o_spec],
        out_specs=[pl.BlockSpec((1, tm, d), lambda b, i: (b, i, 0)), i_spec, o_spec],
        out_shape=[jax.ShapeDtypeStruct((bsz, s, d), F32), jax.ShapeDtypeStruct(w2_in.shape, BF16),
                   jax.ShapeDtypeStruct(w2_out.shape, BF16)],
        scratch_shapes=[pltpu.VMEM((tm, w), BF16)],
        compiler_params=pltpu.CompilerParams(dimension_semantics=("arbitrary", "arbitrary"),
                                             vmem_limit_bytes=VMEM_LIMIT),
        name="outproj",
    )(x, mods, hx, u, vn, gw, gb, wt, wb, w2_in, w2_out)


def _gate_rows(zgc, zg):
    def rows(z):
        bsz, _, t = z.shape
        return z.reshape(bsz, 4, HEADS, t // CHUNK, CHUNK)
    gc = rows(zgc)
    assert gc.shape[3] <= CTX_ROWS
    gc = jnp.pad(gc, ((0, 0), (0, 0), (0, 0), (0, CTX_ROWS - gc.shape[3]), (0, 0)))
    return jnp.concatenate([gc, rows(zg)], axis=3)


def kernel(x, c, ctx, c_ctx, w_ada, b_ada, norm_ffn1, w_ffn1_in, w_ffn1_out, norm_mix, w_in, conv_w, conv_b, b_igate, b_fgate, mlstm_norm, gmlp_norm, gmlp_w, gmlp_b, w_out, norm_ffn2, w_ffn2_in, w_ffn2_out, final_norm):
    bsz, s, d = x.shape
    depth = w_ada.shape[0]
    assert depth == 1, "context-stream update for non-final layers is not implemented"
    l = 0
    qk_w = 2 * HEADS * DK
    mw = HEADS * DV
    c_qk, c_v, c_o = qk_w, qk_w + mw, qk_w + 2 * mw
    c_gate = c_o + 4 * HEADS

    lat_row = lambda b: b
    ctx_row = lambda b: bsz

    rows = -(-(bsz + 1) // 8) * 8
    cond = jnp.zeros((rows, d), F32).at[:bsz].set(c).at[bsz].set(c_ctx)
    mods = _adaln(cond, w_ada[l], b_ada[l]).reshape(rows, N_MOD, d)

    w1i, w1o = w_ffn1_in[l].astype(BF16), w_ffn1_out[l].astype(BF16)
    cw = jnp.pad(conv_w[l], ((0, 8 - CONV_K), (0, 0)))
    cb = conv_b[l].reshape(1, qk_w)
    wout = w_out[l].astype(BF16)

    x, wi = _ffn(x, mods, lat_row, 0, norm_ffn1[l], w1i, w1o, side=w_in[l])
    wm = wi[:, c_gate:]
    ctx = _ffn(ctx.reshape(1, -1, d), mods, ctx_row, 0, norm_ffn1[l], w1i, w1o).reshape(ctx.shape)

    q, kt, v, zg, o, u, vn = _inproj(x, mods, lat_row, norm_mix[l], wi, cw, cb,
                                     wm, gmlp_norm[l].reshape(1, -1))
    ktc, vc, zgc = _inproj(ctx, mods, ctx_row, norm_mix[l], wi, cw[:, qk_w // 2:], cb[:, qk_w // 2:])

    bias = jnp.stack([b_igate[l][0], b_fgate[l][0], b_igate[l][1], b_fgate[l][1]], axis=-1)
    hx = _mlstm(q, kt, v, o, _gate_rows(zgc, zg), ktc, vc, bias.reshape(HEADS, 4, 1, 1),
                mlstm_norm[l].reshape(1, mw))

    gb = jnp.repeat(gmlp_b[l].T, GD, axis=1)
    x, w2i, w2o = _outproj(x, mods, hx, u, vn, gmlp_w[l].astype(BF16), gb, wout[:mw], wout[mw:],
                           w_ffn2_in[l], w_ffn2_out[l])

    return _ffn(x, mods, lat_row, 6, norm_ffn2[l], w2i, w2o, final_g=final_norm)
```

```python
import functools

import jax
import jax.numpy as jnp
from jax import lax
from jax.experimental import pallas as pl
from jax.experimental.pallas import tpu as pltpu

F32 = jnp.float32
BF16 = jnp.bfloat16

EPS = 1e-6
N_MOD = 9
HEADS = 4
DK = 128
DV = 256
CHUNK = 128
GROUPS = 8
GD = 128
CONV_K = 5
HALO = 16
COL_TILE = 256
LOOKAHEAD = 1
LANE = 128
SIDE_ROWS = 512
GATE_PAD = LANE
GATE_ROWS = 4 * HEADS
ROW_SLAB = 16
CTX_ROWS = 8
NORM_COLS = LANE

VMEM_LIMIT = 56 * 1024 * 1024


def _sigmoid(x):
    return 1.0 / (1.0 + jnp.exp(-x))


def _log_sigmoid(x):
    return jnp.minimum(x, 0.0) - jnp.log(1.0 + jnp.exp(-jnp.abs(x)))


def _gelu_tanh(x):
    c2 = -2.0 * 0.7978845608028654
    return x / (1.0 + jnp.exp(x * (c2 + (c2 * 0.044715) * (x * x))))


def _rms(x, g):
    return x * lax.rsqrt(jnp.mean(x * x, axis=-1, keepdims=True) + EPS) * g


def _norm_mod(x, g, shift, scale):
    return _rms(x, g) * (1.0 + scale) + shift


def _dot(a, b):
    return jnp.dot(a, b, preferred_element_type=F32)


def _ada_kernel(c_ref, w_ref, b_ref, o_ref):
    c = c_ref[...]
    s = (c * _sigmoid(c)).astype(BF16)
    o_ref[...] = _dot(s, w_ref[...].astype(BF16)) + b_ref[...]


def _adaln(cond, w, b, tn=1024):
    rows, d = cond.shape
    n = w.shape[1]
    return pl.pallas_call(
        _ada_kernel,
        grid=(n // tn,),
        in_specs=[pl.BlockSpec((rows, d), lambda j: (0, 0)),
                  pl.BlockSpec((d, tn), lambda j: (0, j)),
                  pl.BlockSpec((1, tn), lambda j: (0, j))],
        out_specs=pl.BlockSpec((rows, tn), lambda j: (0, j)),
        out_shape=jax.ShapeDtypeStruct((rows, n), F32),
        compiler_params=pltpu.CompilerParams(dimension_semantics=("parallel",),
                                             vmem_limit_bytes=VMEM_LIMIT),
        name="adaln",
    )(cond, w, b.reshape(1, n))


def _ffn_kernel(*refs, mod_base, final, side):
    refs = list(refs)
    x_ref, mod_ref, g_ref, wg_ref, wu_ref, wo_ref = refs[:6]
    del refs[:6]
    fn_ref = refs.pop(0) if final else None
    sa_ref, sb_ref = (refs.pop(0), refs.pop(0)) if side else (None, None)
    o_ref = refs.pop(0)
    head_ref, tail_ref = (refs.pop(0), refs.pop(0)) if side else (None, None)
    xn_ref = refs.pop(0)
    r_ref = refs.pop(0) if final else None
    f = pl.program_id(2)
    tm = xn_ref.shape[0]

    if side:
        n_head, n_tail, shift = side
        k = (pl.program_id(0) * pl.num_programs(1) + pl.program_id(1)) * pl.num_programs(2) + f

        @pl.when(k < n_head)
        def _():
            head_ref[...] = sa_ref[...].astype(BF16)

        @pl.when((k >= n_head) & (k < n_head + n_tail))
        def _():
            tail_ref[...] = jnp.concatenate([sa_ref[:, shift:], sb_ref[:, :shift]], axis=1).astype(BF16)

    @pl.when(f == 0)
    def _():
        shift = mod_ref[0, mod_base:mod_base + 1, :]
        gain = g_ref[...] * (1.0 + mod_ref[0, mod_base + 1:mod_base + 2, :])

        def slab(i, carry):
            rows = pl.ds(pl.multiple_of(i * ROW_SLAB, ROW_SLAB), ROW_SLAB)
            x = x_ref[0, rows, :]
            r = lax.rsqrt(jnp.mean(x * x, axis=-1, keepdims=True) + EPS)
            xn_ref[rows, :] = (x * r * gain + shift).astype(BF16)
            o_ref[0, rows, :] = jnp.zeros_like(x)
            return carry

        lax.fori_loop(0, tm // ROW_SLAB, slab, 0, unroll=8)

    xn = xn_ref[...]
    gate = _dot(xn, wg_ref[...])
    up = _dot(xn, wu_ref[...])
    h = (gate * _sigmoid(gate) * up).astype(BF16)
    o_ref[0] += _dot(h, wo_ref[...])

    @pl.when(f == pl.num_programs(2) - 1)
    def _():
        half_gate = 0.5 * mod_ref[0, mod_base + 2:mod_base + 3, :]

        def slab(i, carry):
            rows = pl.ds(pl.multiple_of(i * ROW_SLAB, ROW_SLAB), ROW_SLAB)
            y = x_ref[0, rows, :] + half_gate * o_ref[0, rows, :]
            o_ref[0, rows, :] = y
            if final:
                r_ref[rows, :] = lax.rsqrt(jnp.mean(y * y, axis=-1, keepdims=True) + EPS)
            return carry

        lax.fori_loop(0, tm // ROW_SLAB, slab, 0, unroll=8)

        if final:
            def scale(i, carry):
                rows = pl.ds(pl.multiple_of(i * ROW_SLAB, ROW_SLAB), ROW_SLAB)
                o_ref[0, rows, :] = o_ref[0, rows, :] * r_ref[rows, :] * fn_ref[...]
                return carry

            lax.fori_loop(0, tm // ROW_SLAB, scale, 0, unroll=8)


def _ffn(x, mods, mod_row, mod_base, g, w_in, w_out, final_g=None, side=None, tm=1024, tf=512):
    bsz, s, d = x.shape
    ff = w_out.shape[0]
    tm = min(tm, s)
    nf = ff // tf
    final = final_g is not None
    in_specs = [pl.BlockSpec((1, tm, d), lambda b, i, f: (b, i, 0)),
                pl.BlockSpec((1, N_MOD, d), lambda b, i, f: (mod_row(b), 0, 0)),
                pl.BlockSpec((1, d), lambda b, i, f: (0, 0)),
                pl.BlockSpec((d, tf), lambda b, i, f: (0, f)),
                pl.BlockSpec((d, tf), lambda b, i, f: (0, nf + f)),
                pl.BlockSpec((tf, d), lambda b, i, f: (f, 0))]
    args = [x, mods, g.reshape(1, d), w_in, w_in, w_out]
    if final:
        in_specs.append(pl.BlockSpec((1, d), lambda b, i, f: (0, 0)))
        args.append(final_g.reshape(1, d))
    out_specs = [pl.BlockSpec((1, tm, d), lambda b, i, f: (b, i, 0))]
    out_shape = [jax.ShapeDtypeStruct((bsz, s, d), F32)]
    side_cfg = None
    if side is not None:
        w, tail_start = side
        rows, width = w.shape
        c0, shift = divmod(tail_start, LANE)
        nrb = rows // SIDE_ROWS
        head_w = (c0 + 1) * LANE
        tail_w = width - tail_start
        n_head, n_tail = nrb * (head_w // LANE), nrb * (tail_w // LANE)
        assert shift > 0 and nrb * SIDE_ROWS == rows and tail_w % LANE == 0 and head_w <= width
        assert n_head + n_tail <= bsz * (s // tm) * nf
        side_cfg = (n_head, n_tail, shift)

        def step(b, i, f):
            k = (b * (s // tm) + i) * nf + f
            return k, jnp.minimum(k, n_head - 1), jnp.clip(k - n_head, 0, n_tail - 1)

        def a_map(b, i, f):
            k, kh, kt = step(b, i, f)
            return (jnp.where(k < n_head, kh % nrb, kt % nrb), jnp.where(k < n_head, kh // nrb, c0 + kt // nrb))

        def b_map(b, i, f):
            _, _, kt = step(b, i, f)
            return (kt % nrb, c0 + 1 + kt // nrb)

        def head_map(b, i, f):
            _, kh, _ = step(b, i, f)
            return (kh % nrb, kh // nrb)

        def tail_map(b, i, f):
            _, _, kt = step(b, i, f)
            return (kt % nrb, kt // nrb)

        in_specs += [pl.BlockSpec((SIDE_ROWS, LANE), a_map), pl.BlockSpec((SIDE_ROWS, LANE), b_map)]
        args += [w, w]
        out_specs += [pl.BlockSpec((SIDE_ROWS, LANE), head_map), pl.BlockSpec((SIDE_ROWS, LANE), tail_map)]
        out_shape += [jax.ShapeDtypeStruct((rows, head_w), BF16), jax.ShapeDtypeStruct((rows, tail_w), BF16)]
    res = pl.pallas_call(
        functools.partial(_ffn_kernel, mod_base=mod_base, final=final, side=side_cfg),
        grid=(bsz, s // tm, nf),
        in_specs=in_specs,
        out_specs=out_specs,
        out_shape=out_shape,
        scratch_shapes=[pltpu.VMEM((tm, d), BF16)] + ([pltpu.VMEM((tm, 1), F32)] if final else []),
        compiler_params=pltpu.CompilerParams(
            dimension_semantics=("arbitrary", "arbitrary", "arbitrary"),
            vmem_limit_bytes=VMEM_LIMIT),
        name="ffn_final" if final else "ffn",
    )(*args)
    return res if side is not None else res[0]


def _inproj_kernel(*refs, ctx_mode, tm):
    if ctx_mode:
        (x_ref, xp_ref, xq_ref, mod_ref, g_ref, wqk_ref, wv_ref, wg_ref, cw_ref, cb_ref,
         k_ref, v_ref, zg_ref, zs_ref, xn_ref, zgs_ref) = refs
    else:
        (x_ref, xp_ref, xq_ref, mod_ref, g_ref, wqk_ref, wv_ref, wg_ref, cw_ref, cb_ref,
         wo_ref, wm_ref, gn_ref,
         q_ref, k_ref, v_ref, zg_ref, o_ref, u_ref, vn_ref, zs_ref, xn_ref, zgs_ref, gv_ref) = refs
    i = pl.program_id(1)
    g = g_ref[...]
    shift = mod_ref[0, 3:4, :]
    gain = g * (1.0 + mod_ref[0, 4:5, :])

    def norm_rows(x):
        r = lax.rsqrt(jnp.mean(x * x, axis=-1, keepdims=True) + EPS)
        return (x * r * gain + shift).astype(BF16)

    xn_ref[0:HALO, :] = norm_rows(xp_ref[0])
    xn_ref[HALO + tm:, :] = norm_rows(xq_ref[0])

    def slab(t, carry):
        rows = pl.ds(pl.multiple_of(t * ROW_SLAB, ROW_SLAB), ROW_SLAB)
        xn_ref[pl.ds(pl.multiple_of(HALO + t * ROW_SLAB, ROW_SLAB), ROW_SLAB), :] = norm_rows(x_ref[0, rows, :])
        return carry

    lax.fori_loop(0, tm // ROW_SLAB, slab, 0, unroll=8)
    xn = xn_ref[HALO:HALO + tm, :]

    def col(t):
        return slice(t * COL_TILE, (t + 1) * COL_TILE)

    tasks = []
    ssq = [jnp.zeros((tm, 1), F32)]

    if not ctx_mode:
        gw = wm_ref.shape[1] // 2

        def gv_epilogue(z, t):
            gv = _gelu_tanh(z)
            gv_ref[:, col(t)] = gv
            ssq[0] = ssq[0] + jnp.sum(gv * gv, axis=-1, keepdims=True)

        def u_epilogue(z, t):
            u_ref[0, :, col(t)] = _gelu_tanh(z).astype(BF16)

        for t in range(gw // COL_TILE):
            tasks.append((functools.partial(lambda t: _dot(xn, wm_ref[:, gw + t * COL_TILE:gw + (t + 1) * COL_TILE]), t),
                          functools.partial(gv_epilogue, t=t)))
        for t in range(gw // COL_TILE):
            tasks.append((functools.partial(lambda t: _dot(xn, wm_ref[:, col(t)]), t),
                          functools.partial(u_epilogue, t=t)))

    wq = wqk_ref.shape[1]
    k_off = 0 if ctx_mode else wq // 2

    def qk_epilogue(z, t):
        zs = zs_ref.at[t % 2]
        zs[...] = z
        zs[0:HALO, :] = jnp.where(i == 0, 0.0, z[0:HALO])
        zs[HALO + tm:, :] = jnp.where(i == pl.num_programs(1) - 1, 0.0, z[HALO + tm:])
        conv = cb_ref[:, col(t)]
        for j in range(CONV_K):
            off = HALO - CONV_K // 2 + j
            conv = conv + zs[off:off + tm, :] * cw_ref[j:j + 1, col(t)]
        qk = conv * _sigmoid(conv)
        for hh in range(COL_TILE // DK):
            c0 = t * COL_TILE + hh * DK
            if c0 < k_off:
                q_ref[0, c0 // DK] = (qk[:, hh * DK:(hh + 1) * DK] * (DK ** -0.5)).astype(BF16)
            else:
                for cc in range(tm // CHUNK):
                    k_ref[0, (c0 - k_off) // DK, cc] = (
                        qk[cc * CHUNK:(cc + 1) * CHUNK, hh * DK:(hh + 1) * DK].T.astype(BF16))

    for t in range(wq // COL_TILE):
        tasks.append((functools.partial(lambda t: _dot(xn_ref[...], wqk_ref[:, col(t)]), t),
                      functools.partial(qk_epilogue, t=t)))

    if not ctx_mode:
        def o_epilogue(z, h):
            o_ref[0, h] = _sigmoid(z).astype(BF16)

        for h in range(HEADS):
            tasks.append((functools.partial(lambda h: _dot(xn, wo_ref[:, h * DV:(h + 1) * DV]), h),
                          functools.partial(o_epilogue, h=h)))

    def v_epilogue(z, h):
        if h == 0 and not ctx_mode:
            r = lax.rsqrt(ssq[0] * (1.0 / gw) + EPS)
            vn_ref[0] = (gv_ref[...] * r * gn_ref[...]).astype(BF16)
        v_ref[0, h] = z.astype(BF16)

    for h in range(HEADS):
        tasks.append((functools.partial(lambda h: _dot(xn, wv_ref[:, h * DV:(h + 1) * DV]), h),
                      functools.partial(v_epilogue, h=h)))

    def gate_epilogue(z):
        zgs_ref[...] = z
        zg_ref[0] = zgs_ref[...].T[:GATE_ROWS, :]

    tasks.append((lambda: _dot(xn, wg_ref[...]), gate_epilogue))

    pending = [product() for product, _ in tasks[:LOOKAHEAD]]
    for n, (_, epilogue) in enumerate(tasks):
        if n + LOOKAHEAD < len(tasks):
            pending.append(tasks[n + LOOKAHEAD][0]())
        epilogue(pending.pop(0))


def _const_spec(shape):
    nd = len(shape)
    return pl.BlockSpec(shape, lambda b, i: (0,) * nd, pipeline_mode=pl.Buffered(1))


def _inproj(x, mods, mod_row, g, wi, cw, cb, wm=None, gn=None, tm=512):
    bsz, s, d = x.shape
    ctx_mode = wm is None
    tm = min(tm, s)
    nh = s // HALO
    r = tm // HALO
    qk_w, mw = 2 * HEADS * DK, HEADS * DV
    assert qk_w == mw and (qk_w + 2 * mw) % GATE_PAD == 0
    wq = qk_w // 2 if ctx_mode else qk_w

    def wcols(width, block):
        return pl.BlockSpec((d, width), lambda b, i: (0, block), pipeline_mode=pl.Buffered(1))

    in_specs = [pl.BlockSpec((1, tm, d), lambda b, i: (b, i, 0)),
                pl.BlockSpec((1, HALO, d), lambda b, i: (b, jnp.maximum(i * r - 1, 0), 0)),
                pl.BlockSpec((1, HALO, d), lambda b, i: (b, jnp.minimum((i + 1) * r, nh - 1), 0)),
                pl.BlockSpec((1, N_MOD, d), lambda b, i: (mod_row(b), 0, 0)),
                _const_spec((1, d)),
                wcols(wq, 1 if ctx_mode else 0),
                wcols(mw, 1),
                wcols(GATE_PAD, (qk_w + 2 * mw) // GATE_PAD),
                _const_spec(cw.shape), _const_spec(cb.shape)]
    args = [x, x, x, mods, g.reshape(1, d), wi, wi, wi, cw, cb]

    def tok(width, dtype):
        return (pl.BlockSpec((1, tm, width), lambda b, i: (b, i, 0)),
                jax.ShapeDtypeStruct((bsz, s, width), dtype))

    kt = (pl.BlockSpec((1, HEADS, tm // CHUNK, DK, CHUNK), lambda b, i: (b, 0, i, 0, 0)),
          jax.ShapeDtypeStruct((bsz, HEADS, s // CHUNK, DK, CHUNK), BF16))
    zgt = (pl.BlockSpec((1, GATE_ROWS, tm), lambda b, i: (b, 0, i)),
           jax.ShapeDtypeStruct((bsz, GATE_ROWS, s), F32))
    def heads(width):
        return (pl.BlockSpec((1, HEADS, tm, width), lambda b, i: (b, 0, i, 0)),
                jax.ShapeDtypeStruct((bsz, HEADS, s, width), BF16))

    if ctx_mode:
        outs = [kt, heads(DV), zgt]
    else:
        in_specs += [wcols(mw, 2), _const_spec(wm.shape), _const_spec(gn.shape)]
        args += [wi, wm, gn]
        outs = [heads(DK), kt, heads(DV), zgt,
                heads(DV), tok(wm.shape[1] // 2, BF16), tok(wm.shape[1] // 2, BF16)]
    return pl.pallas_call(
        functools.partial(_inproj_kernel, ctx_mode=ctx_mode, tm=tm),
        grid=(bsz, s // tm),
        in_specs=in_specs,
        out_specs=[o[0] for o in outs],
        out_shape=[o[1] for o in outs],
        scratch_shapes=[pltpu.VMEM((2, tm + 2 * HALO, COL_TILE), F32), pltpu.VMEM((tm + 2 * HALO, d), BF16),
                        pltpu.VMEM((tm, GATE_PAD), F32)]
        + ([] if ctx_mode else [pltpu.VMEM((tm, wm.shape[1] // 2), F32)]),
        compiler_params=pltpu.CompilerParams(dimension_semantics=("parallel", "parallel"),
                                             vmem_limit_bytes=VMEM_LIMIT),
        name="inproj_ctx" if ctx_mode else "inproj",
    )(*args)


def _lane_scan(x, op, ident, reverse):
    lane = lax.broadcasted_iota(jnp.int32, x.shape, 1)
    n = x.shape[1]
    k = 1
    while k < n:
        if reverse:
            x = op(x, jnp.where(lane < n - k, pltpu.roll(x, n - k, axis=1), ident))
        else:
            x = op(x, jnp.where(lane >= k, pltpu.roll(x, k, axis=1), ident))
        k *= 2
    return x


def _mlstm_kernel(q_ref, kt_ref, v_ref, o_ref, g_ref, ktc_ref, vc_ref, br_ref, ng_ref, out_ref,
                  e_ref, col_ref, tot_ref, emax_ref, rhs_ref, ms_ref, c_ref, m_ref, lhs_ref, floor_ref):
    s = q_ref.shape[2]
    nc = s // CHUNK
    ncc = vc_ref.shape[2] // CHUNK
    row = lax.broadcasted_iota(jnp.int32, (CHUNK, CHUNK), 0)
    col = lax.broadcasted_iota(jnp.int32, (CHUNK, CHUNK), 1)
    eye = row == col
    ones = jnp.ones((CHUNK, NORM_COLS), BF16)
    sel_r = lax.broadcasted_iota(jnp.int32, (3 * CHUNK, 2 * NORM_COLS), 0)
    sel_c = lax.broadcasted_iota(jnp.int32, (3 * CHUNK, 2 * NORM_COLS), 1)
    sel = jnp.where((sel_r < CHUNK) == (sel_c < NORM_COLS), 1.0, 0.0).astype(BF16)

    for d in range(2):
        logi = g_ref[0, 2 * d, 0] + br_ref[0, 2 * d]
        logf = _log_sigmoid(g_ref[0, 2 * d + 1, 0] + br_ref[0, 2 * d + 1])
        b = _lane_scan(logf, jnp.add, 0.0, d == 1)
        e = logi - b
        e_ref[d] = e
        b_hi = b.astype(BF16).astype(F32)
        col_ref[d, 0] = _lane_scan(e, jnp.maximum, -jnp.inf, d == 1).astype(BF16).astype(F32)
        col_ref[d, 1] = b_hi
        col_ref[d, 2] = (b - b_hi).astype(BF16).astype(F32)
        tot_ref[d] = jnp.sum(logf, axis=-1, keepdims=True)
        emax_ref[d] = jnp.max(e, axis=-1, keepdims=True)

    def scan_step(d, r, kt, v):
        e_r = e_ref[d, pl.ds(r, 1), :]
        emax = emax_ref[d, pl.ds(r, 1), :]
        kw = (kt.astype(F32) * jnp.exp(e_r - emax)).astype(BF16)
        v_aug = jnp.concatenate([v, ones], axis=1)
        upd = _dot(kw, v_aug)
        m = m_ref[d]
        c = c_ref[d]
        mx = jnp.maximum(m, emax)
        c_ref[d] = jnp.exp(m - mx) * c + jnp.exp(emax - mx) * upd
        m_ref[d] = tot_ref[d, pl.ds(r, 1), :] + mx
        return c.astype(BF16), m, v_aug

    c_ref[...] = jnp.zeros_like(c_ref)
    m_ref[...] = jnp.zeros_like(m_ref)
    for t in range(ncc):
        for d in range(2):
            ci = t if d == 0 else ncc - 1 - t
            scan_step(d, ci, ktc_ref[0, 0, ci], vc_ref[0, 0, ci * CHUNK:(ci + 1) * CHUNK, :])

    def scan_body(t, carry):
        for d in range(2):
            ci = t if d == 0 else nc - 1 - t
            rows = pl.ds(pl.multiple_of(ci * CHUNK, CHUNK), CHUNK)
            c_in, m_in, v_aug = scan_step(d, CTX_ROWS + ci, kt_ref[0, 0, ci], v_ref[0, 0, rows, :])
            rhs_ref[d, ci, 0:DK, :] = c_in
            rhs_ref[d, ci, DK:, :] = v_aug
            ms_ref[d, ci] = m_in
        return carry

    lax.fori_loop(0, nc, scan_body, 0, unroll=4)

    def weights(ci, slot):
        q = q_ref[0, 0, pl.ds(pl.multiple_of(ci * CHUNK, CHUNK), CHUNK), :]
        qf = q.astype(F32)
        qk = _dot(q, kt_ref[0, 0, ci])
        r = pl.ds(CTX_ROWS + ci, 1)
        for d in range(2):
            inc = (col <= row) if d == 0 else (col >= row)
            e_m = jnp.where(inc, e_ref[d, r, :], -jnp.inf)
            diag = jnp.concatenate([jnp.where(eye, col_ref[d, i, r, :], 0.0) for i in range(3)], axis=1)
            rep = _dot(diag.astype(BF16), sel)
            m = ms_ref[d, ci]
            mm = jnp.maximum(m, rep[:, :NORM_COLS])
            p = (qk * jnp.exp(e_m - mm)).astype(BF16)
            qa = (qf * jnp.exp(m - mm)).astype(BF16)
            lhs_ref[slot, d] = jnp.concatenate([qa, p], axis=1)
            floor_ref[slot, d] = jnp.exp(-(rep[:, NORM_COLS:] + mm))

    def apply(ci, slot):
        rows = pl.ds(pl.multiple_of(ci * CHUNK, CHUNK), CHUNK)
        h = None
        for d in range(2):
            comb = _dot(lhs_ref[slot, d], rhs_ref[d, ci])
            inv = 1.0 / jnp.maximum(jnp.abs(comb[:, DV:]), floor_ref[slot, d])
            hd = comb[:, :DV] * jnp.concatenate([inv] * (DV // NORM_COLS), axis=1)
            h = hd if h is None else h + hd
        out_ref[0, 0, rows, :] = (_rms(h, ng_ref[...]) * o_ref[0, 0, rows, :].astype(F32)).astype(BF16)

    def out_body(ci, carry):
        apply(ci - 1, (ci - 1) % 2)
        weights(jnp.minimum(ci, nc - 1), ci % 2)
        return carry

    weights(0, 0)
    lax.fori_loop(1, nc + 1, out_body, 0, unroll=4)


def _mlstm(q, kt, v, o, g, ktc, vc, bias, norm_g):
    bsz, _, s, _ = q.shape
    sc = vc.shape[2]
    nc, ncc = s // CHUNK, sc // CHUNK
    nr = g.shape[3]
    return pl.pallas_call(
        _mlstm_kernel,
        grid=(bsz, HEADS),
        in_specs=[pl.BlockSpec((1, 1, s, DK), lambda b, h: (b, h, 0, 0)),
                  pl.BlockSpec((1, 1, nc, DK, CHUNK), lambda b, h: (b, h, 0, 0, 0)),
                  pl.BlockSpec((1, 1, s, DV), lambda b, h: (b, h, 0, 0)),
                  pl.BlockSpec((1, 1, s, DV), lambda b, h: (b, h, 0, 0)),
                  pl.BlockSpec((1, 4, 1, nr, CHUNK), lambda b, h: (b, 0, h, 0, 0)),
                  pl.BlockSpec((1, 1, ncc, DK, CHUNK), lambda b, h: (b, h, 0, 0, 0)),
                  pl.BlockSpec((1, 1, sc, DV), lambda b, h: (b, h, 0, 0)),
                  pl.BlockSpec((1, 4, 1, 1), lambda b, h: (h, 0, 0, 0)),
                  pl.BlockSpec((1, DV), lambda b, h: (0, h))],
        out_specs=pl.BlockSpec((1, 1, s, DV), lambda b, h: (b, h, 0, 0)),
        out_shape=jax.ShapeDtypeStruct((bsz, HEADS, s, DV), BF16),
        scratch_shapes=[pltpu.VMEM((2, nr, CHUNK), F32), pltpu.VMEM((2, 3, nr, CHUNK), F32),
                        pltpu.VMEM((2, nr, 1), F32), pltpu.VMEM((2, nr, 1), F32),
                        pltpu.VMEM((2, nc, DK + CHUNK, DV + NORM_COLS), BF16), pltpu.VMEM((2, nc, 1, 1), F32),
                        pltpu.VMEM((2, DK, DV + NORM_COLS), F32), pltpu.VMEM((2, 1, 1), F32),
                        pltpu.VMEM((2, 2, CHUNK, DK + CHUNK), BF16), pltpu.VMEM((2, 2, CHUNK, NORM_COLS), F32)],
        compiler_params=pltpu.CompilerParams(dimension_semantics=("parallel", "parallel"),
                                             vmem_limit_bytes=VMEM_LIMIT),
        name="mlstm",
    )(q, kt, v, o, g, ktc, vc, bias, norm_g)


def _outproj_kernel(x_ref, mod_ref, hx_ref, u_ref, vn_ref, gw_ref, gb_ref, wt_ref, wb_ref, w2i_ref, w2o_ref,
                    o_ref, w2i_bf_ref, w2o_bf_ref, gx_ref, *, cast_steps):
    tm = x_ref.shape[1]

    @pl.when(pl.program_id(0) * pl.num_programs(1) + pl.program_id(1) < cast_steps)
    def _():
        w2i_bf_ref[...] = w2i_ref[...].astype(BF16)
        w2o_bf_ref[...] = w2o_ref[...].astype(BF16)

    for ci in range(tm // CHUNK):
        rows = slice(ci * CHUNK, (ci + 1) * CHUNK)
        for g in range(GROUPS):
            cols = slice(g * GD, (g + 1) * GD)
            sp = _dot(gw_ref[g], vn_ref[0, rows, cols]) + gb_ref[:, cols]
            gx_ref[rows, cols] = (u_ref[0, rows, cols].astype(F32) * sp).astype(BF16)
    y = _dot(gx_ref[...], wb_ref[...])
    for h in range(HEADS):
        y = y + _dot(hx_ref[0, h], wt_ref[h * DV:(h + 1) * DV, :])
    o_ref[0] = x_ref[0] + mod_ref[0, 5:6, :] * y


def _cast_block(extent, max_blocks):
    for blk in range(LANE, extent + 1, LANE):
        if extent % blk == 0 and extent // blk <= max_blocks:
            return blk
    raise ValueError(f"no block of {extent} fits {max_blocks} grid steps")


def _outproj(x, mods, hx, u, vn, gw, gb, wt, wb, w2_in, w2_out, tm=512):
    bsz, s, d = x.shape
    w = u.shape[2]
    steps = bsz * (s // tm)
    cast_cols = _cast_block(w2_in.shape[1], steps)
    cast_rows = _cast_block(w2_out.shape[0], steps)
    ci, ro = w2_in.shape[1] // cast_cols, w2_out.shape[0] // cast_rows
    i_spec = pl.BlockSpec((w2_in.shape[0], cast_cols), lambda b, i: (0, jnp.minimum(b * (s // tm) + i, ci - 1)))
    o_spec = pl.BlockSpec((cast_rows, w2_out.shape[1]), lambda b, i: (jnp.minimum(b * (s // tm) + i, ro - 1), 0))
    return pl.pallas_call(
        functools.partial(_outproj_kernel, cast_steps=max(ci, ro)),
        grid=(bsz, s // tm),
        in_specs=[pl.BlockSpec((1, tm, d), lambda b, i: (b, i, 0)),
                  pl.BlockSpec((1, N_MOD, d), lambda b, i: (b, 0, 0)),
                  pl.BlockSpec((1, HEADS, tm, DV), lambda b, i: (b, 0, i, 0)),
                  pl.BlockSpec((1, tm, w), lambda b, i: (b, i, 0)),
                  pl.BlockSpec((1, tm, w), lambda b, i: (b, i, 0)),
                  _const_spec(gw.shape), _const_spec(gb.shape),
                  _const_spec(wt.shape), _const_spec(wb.shape), i_spec, o_spec],
        out_specs=[pl.BlockSpec((1, tm, d), lambda b, i: (b, i, 0)), i_spec, o_spec],
        out_shape=[jax.ShapeDtypeStruct((bsz, s, d), F32), jax.ShapeDtypeStruct(w2_in.shape, BF16),
                   jax.ShapeDtypeStruct(w2_out.shape, BF16)],
        scratch_shapes=[pltpu.VMEM((tm, w), BF16)],
        compiler_params=pltpu.CompilerParams(dimension_semantics=("arbitrary", "arbitrary"),
                                             vmem_limit_bytes=VMEM_LIMIT),
        name="outproj",
    )(x, mods, hx, u, vn, gw, gb, wt, wb, w2_in, w2_out)


def _gate_rows(zgc, zg):
    def rows(z):
        bsz, _, t = z.shape
        return z.reshape(bsz, 4, HEADS, t // CHUNK, CHUNK)
    gc = rows(zgc)
    assert gc.shape[3] <= CTX_ROWS
    gc = jnp.pad(gc, ((0, 0), (0, 0), (0, 0), (0, CTX_ROWS - gc.shape[3]), (0, 0)))
    return jnp.concatenate([gc, rows(zg)], axis=3)


def kernel(x, c, ctx, c_ctx, w_ada, b_ada, norm_ffn1, w_ffn1_in, w_ffn1_out, norm_mix, w_in, conv_w, conv_b, b_igate, b_fgate, mlstm_norm, gmlp_norm, gmlp_w, gmlp_b, w_out, norm_ffn2, w_ffn2_in, w_ffn2_out, final_norm):
    bsz, s, d = x.shape
    depth = w_ada.shape[0]
    assert depth == 1, "context-stream update for non-final layers is not implemented"
    l = 0
    qk_w = 2 * HEADS * DK
    mw = HEADS * DV
    c_qk, c_v, c_o = qk_w, qk_w + mw, qk_w + 2 * mw
    c_gate = c_o + 4 * HEADS

    lat_row = lambda b: b
    ctx_row = lambda b: bsz

    rows = -(-(bsz + 1) // 8) * 8
    cond = jnp.zeros((rows, d), F32).at[:bsz].set(c).at[bsz].set(c_ctx)
    mods = _adaln(cond, w_ada[l], b_ada[l]).reshape(rows, N_MOD, d)

    w1i, w1o = w_ffn1_in[l].astype(BF16), w_ffn1_out[l].astype(BF16)
    cw = jnp.pad(conv_w[l], ((0, 8 - CONV_K), (0, 0)))
    cb = conv_b[l].reshape(1, qk_w)
    wout = w_out[l].astype(BF16)

    x, wi, wm = _ffn(x, mods, lat_row, 0, norm_ffn1[l], w1i, w1o, side=(w_in[l], c_gate))
    ctx = _ffn(ctx.reshape(1, -1, d), mods, ctx_row, 0, norm_ffn1[l], w1i, w1o).reshape(ctx.shape)

    q, kt, v, zg, o, u, vn = _inproj(x, mods, lat_row, norm_mix[l], wi, cw, cb,
                                     wm, gmlp_norm[l].reshape(1, -1))
    ktc, vc, zgc = _inproj(ctx, mods, ctx_row, norm_mix[l], wi, cw[:, qk_w // 2:], cb[:, qk_w // 2:])

    bias = jnp.stack([b_igate[l][0], b_fgate[l][0], b_igate[l][1], b_fgate[l][1]], axis=-1)
    hx = _mlstm(q, kt, v, o, _gate_rows(zgc, zg), ktc, vc, bias.reshape(HEADS, 4, 1, 1),
                mlstm_norm[l].reshape(1, mw))

    gb = jnp.repeat(gmlp_b[l].T, GD, axis=1)
    x, w2i, w2o = _outproj(x, mods, hx, u, vn, gmlp_w[l].astype(BF16), gb, wout[:mw], wout[mw:],
                           w_ffn2_in[l], w_ffn2_out[l])

    return _ffn(x, mods, lat_row, 6, norm_ffn2[l], w2i, w2o, final_g=final_norm)
```

```python
import functools

import jax
import jax.numpy as jnp
from jax import lax
from jax.experimental import pallas as pl
from jax.experimental.pallas import tpu as pltpu

F32 = jnp.float32
BF16 = jnp.bfloat16

EPS = 1e-6
N_MOD = 9
HEADS = 4
DK = 128
DV = 256
CHUNK = 128
GROUPS = 8
GD = 128
CONV_K = 5

LANE = 128
SUBLANE = 8
MXU_WIDTH = 256
VMEM_LIMIT = 56 * 1024 * 1024

HALO = 2 * SUBLANE
ROW_SLAB = 2 * SUBLANE
COL_TILE = MXU_WIDTH
LOOKAHEAD = 1
GATE_PAD = LANE
GATE_ROWS = 4 * HEADS
CTX_ROWS = SUBLANE
NORM_COLS = LANE
assert CHUNK == DK == NORM_COLS == LANE and CONV_K // 2 <= HALO


def _sigmoid(x):
    return 1.0 / (1.0 + jnp.exp(-x))


def _log_sigmoid(x):
    return jnp.minimum(x, 0.0) - jnp.log(1.0 + jnp.exp(-jnp.abs(x)))


def _gelu_tanh(x):
    c2 = -2.0 * 0.7978845608028654
    return x / (1.0 + jnp.exp(x * (c2 + (c2 * 0.044715) * (x * x))))


def _rms(x, g):
    return x * lax.rsqrt(jnp.mean(x * x, axis=-1, keepdims=True) + EPS) * g


def _dot(a, b):
    return jnp.dot(a, b, preferred_element_type=F32)


def _ada_kernel(c_ref, w_ref, b_ref, o_ref):
    c = c_ref[...]
    s = (c * _sigmoid(c)).astype(BF16)
    o_ref[...] = _dot(s, w_ref[...].astype(BF16)) + b_ref[...]


def _adaln(cond, w, b, tn=1024):
    rows, d = cond.shape
    n = w.shape[1]
    return pl.pallas_call(
        _ada_kernel,
        grid=(n // tn,),
        in_specs=[pl.BlockSpec((rows, d), lambda j: (0, 0)),
                  pl.BlockSpec((d, tn), lambda j: (0, j)),
                  pl.BlockSpec((1, tn), lambda j: (0, j))],
        out_specs=pl.BlockSpec((rows, tn), lambda j: (0, j)),
        out_shape=jax.ShapeDtypeStruct((rows, n), F32),
        compiler_params=pltpu.CompilerParams(dimension_semantics=("parallel",),
                                             vmem_limit_bytes=VMEM_LIMIT),
        name="adaln",
    )(cond, w, b.reshape(1, n))


def _ffn_kernel(*refs, mod_base, final):
    if final:
        x_ref, mod_ref, g_ref, wg_ref, wu_ref, wo_ref, fn_ref, o_ref, xn_ref, r_ref = refs
    else:
        x_ref, mod_ref, g_ref, wg_ref, wu_ref, wo_ref, o_ref, xn_ref = refs
    f = pl.program_id(2)
    tm = xn_ref.shape[0]

    @pl.when(f == 0)
    def _():
        shift = mod_ref[0, mod_base:mod_base + 1, :]
        gain = g_ref[...] * (1.0 + mod_ref[0, mod_base + 1:mod_base + 2, :])

        def slab(i, carry):
            rows = pl.ds(pl.multiple_of(i * ROW_SLAB, ROW_SLAB), ROW_SLAB)
            x = x_ref[0, rows, :]
            r = lax.rsqrt(jnp.mean(x * x, axis=-1, keepdims=True) + EPS)
            xn_ref[rows, :] = (x * r * gain + shift).astype(BF16)
            o_ref[0, rows, :] = jnp.zeros_like(x)
            return carry

        lax.fori_loop(0, tm // ROW_SLAB, slab, 0, unroll=8)

    xn = xn_ref[...]
    gate = _dot(xn, wg_ref[...])
    up = _dot(xn, wu_ref[...])
    h = (gate * _sigmoid(gate) * up).astype(BF16)
    o_ref[0] += _dot(h, wo_ref[...])

    @pl.when(f == pl.num_programs(2) - 1)
    def _():
        half_gate = 0.5 * mod_ref[0, mod_base + 2:mod_base + 3, :]

        def slab(i, carry):
            rows = pl.ds(pl.multiple_of(i * ROW_SLAB, ROW_SLAB), ROW_SLAB)
            y = x_ref[0, rows, :] + half_gate * o_ref[0, rows, :]
            o_ref[0, rows, :] = y
            if final:
                r_ref[rows, :] = lax.rsqrt(jnp.mean(y * y, axis=-1, keepdims=True) + EPS)
            return carry

        lax.fori_loop(0, tm // ROW_SLAB, slab, 0, unroll=8)

        if final:
            def scale(i, carry):
                rows = pl.ds(pl.multiple_of(i * ROW_SLAB, ROW_SLAB), ROW_SLAB)
                o_ref[0, rows, :] = o_ref[0, rows, :] * r_ref[rows, :] * fn_ref[...]
                return carry

            lax.fori_loop(0, tm // ROW_SLAB, scale, 0, unroll=8)


def _ffn(x, mods, mod_row, mod_base, g, w_in, w_out, final_g=None, tm=1024, tf=512):
    bsz, s, d = x.shape
    ff = w_out.shape[0]
    tm = min(tm, s)
    nf = ff // tf
    final = final_g is not None
    in_specs = [pl.BlockSpec((1, tm, d), lambda b, i, f: (b, i, 0)),
                pl.BlockSpec((1, N_MOD, d), lambda b, i, f: (mod_row(b), 0, 0)),
                pl.BlockSpec((1, d), lambda b, i, f: (0, 0)),
                pl.BlockSpec((d, tf), lambda b, i, f: (0, f)),
                pl.BlockSpec((d, tf), lambda b, i, f: (0, nf + f)),
                pl.BlockSpec((tf, d), lambda b, i, f: (f, 0))]
    args = [x, mods, g.reshape(1, d), w_in, w_in, w_out]
    if final:
        in_specs.append(pl.BlockSpec((1, d), lambda b, i, f: (0, 0)))
        args.append(final_g.reshape(1, d))
    return pl.pallas_call(
        functools.partial(_ffn_kernel, mod_base=mod_base, final=final),
        grid=(bsz, s // tm, nf),
        in_specs=in_specs,
        out_specs=pl.BlockSpec((1, tm, d), lambda b, i, f: (b, i, 0)),
        out_shape=jax.ShapeDtypeStruct((bsz, s, d), F32),
        scratch_shapes=[pltpu.VMEM((tm, d), BF16)] + ([pltpu.VMEM((tm, 1), F32)] if final else []),
        compiler_params=pltpu.CompilerParams(
            dimension_semantics=("parallel", "parallel", "arbitrary"),
            vmem_limit_bytes=VMEM_LIMIT),
        name="ffn_final" if final else "ffn",
    )(*args)


def _inproj_kernel(*refs, ctx_mode, tm):
    if ctx_mode:
        (x_ref, xp_ref, xq_ref, mod_ref, g_ref, wqk_ref, wv_ref, wg_ref, cw_ref, cb_ref,
         k_ref, v_ref, zg_ref, zs_ref, xn_ref, zgs_ref) = refs
    else:
        (x_ref, xp_ref, xq_ref, mod_ref, g_ref, wqk_ref, wv_ref, wg_ref, cw_ref, cb_ref,
         wo_ref, wm_ref, gn_ref,
         q_ref, k_ref, v_ref, zg_ref, o_ref, u_ref, vn_ref, zs_ref, xn_ref, zgs_ref, gv_ref) = refs
    i = pl.program_id(1)
    g = g_ref[...]
    shift = mod_ref[0, 3:4, :]
    gain = g * (1.0 + mod_ref[0, 4:5, :])

    def norm_rows(x):
        r = lax.rsqrt(jnp.mean(x * x, axis=-1, keepdims=True) + EPS)
        return (x * r * gain + shift).astype(BF16)

    xn_ref[0:HALO, :] = norm_rows(xp_ref[0])
    xn_ref[HALO + tm:, :] = norm_rows(xq_ref[0])

    def slab(t, carry):
        rows = pl.ds(pl.multiple_of(t * ROW_SLAB, ROW_SLAB), ROW_SLAB)
        xn_ref[pl.ds(pl.multiple_of(HALO + t * ROW_SLAB, ROW_SLAB), ROW_SLAB), :] = norm_rows(x_ref[0, rows, :])
        return carry

    lax.fori_loop(0, tm // ROW_SLAB, slab, 0, unroll=8)
    xn = xn_ref[HALO:HALO + tm, :]

    def col(t):
        return slice(t * COL_TILE, (t + 1) * COL_TILE)

    tasks = []
    ssq = [jnp.zeros((tm, 1), F32)]

    if not ctx_mode:
        gw = wm_ref.shape[1] // 2

        def gv_epilogue(z, t):
            gv = _gelu_tanh(z)
            gv_ref[:, col(t)] = gv
            ssq[0] = ssq[0] + jnp.sum(gv * gv, axis=-1, keepdims=True)

        def u_epilogue(z, t):
            u_ref[0, :, col(t)] = _gelu_tanh(z).astype(BF16)

        for t in range(gw // COL_TILE):
            tasks.append((functools.partial(lambda t: _dot(xn, wm_ref[:, gw + t * COL_TILE:gw + (t + 1) * COL_TILE]), t),
                          functools.partial(gv_epilogue, t=t)))
        for t in range(gw // COL_TILE):
            tasks.append((functools.partial(lambda t: _dot(xn, wm_ref[:, col(t)]), t),
                          functools.partial(u_epilogue, t=t)))

    wq = wqk_ref.shape[1]
    k_off = 0 if ctx_mode else wq // 2

    def qk_epilogue(z, t):
        zs = zs_ref.at[t % 2]
        zs[...] = z
        zs[0:HALO, :] = jnp.where(i == 0, 0.0, z[0:HALO])
        zs[HALO + tm:, :] = jnp.where(i == pl.num_programs(1) - 1, 0.0, z[HALO + tm:])
        conv = cb_ref[:, col(t)]
        for j in range(CONV_K):
            off = HALO - CONV_K // 2 + j
            conv = conv + zs[off:off + tm, :] * cw_ref[j:j + 1, col(t)]
        qk = conv * _sigmoid(conv)
        for hh in range(COL_TILE // DK):
            c0 = t * COL_TILE + hh * DK
            if c0 < k_off:
                q_ref[0, c0 // DK] = (qk[:, hh * DK:(hh + 1) * DK] * (DK ** -0.5)).astype(BF16)
            else:
                for cc in range(tm // CHUNK):
                    k_ref[0, (c0 - k_off) // DK, cc] = (
                        qk[cc * CHUNK:(cc + 1) * CHUNK, hh * DK:(hh + 1) * DK].T.astype(BF16))

    for t in range(wq // COL_TILE):
        tasks.append((functools.partial(lambda t: _dot(xn_ref[...], wqk_ref[:, col(t)]), t),
                      functools.partial(qk_epilogue, t=t)))

    if not ctx_mode:
        def o_epilogue(z, h):
            o_ref[0, h] = _sigmoid(z).astype(BF16)

        for h in range(HEADS):
            tasks.append((functools.partial(lambda h: _dot(xn, wo_ref[:, h * DV:(h + 1) * DV]), h),
                          functools.partial(o_epilogue, h=h)))

    def v_epilogue(z, h):
        if h == 0 and not ctx_mode:
            r = lax.rsqrt(ssq[0] * (1.0 / gw) + EPS)
            vn_ref[0] = (gv_ref[...] * r * gn_ref[...]).astype(BF16)
        v_ref[0, h] = z.astype(BF16)

    for h in range(HEADS):
        tasks.append((functools.partial(lambda h: _dot(xn, wv_ref[:, h * DV:(h + 1) * DV]), h),
                      functools.partial(v_epilogue, h=h)))

    def gate_epilogue(z):
        zgs_ref[...] = z
        zg_ref[0] = zgs_ref[...].T[:GATE_ROWS, :]

    tasks.append((lambda: _dot(xn, wg_ref[...]), gate_epilogue))

    pending = [product() for product, _ in tasks[:LOOKAHEAD]]
    for n, (_, epilogue) in enumerate(tasks):
        if n + LOOKAHEAD < len(tasks):
            pending.append(tasks[n + LOOKAHEAD][0]())
        epilogue(pending.pop(0))


def _const_spec(shape):
    nd = len(shape)
    return pl.BlockSpec(shape, lambda b, i: (0,) * nd, pipeline_mode=pl.Buffered(1))


def _inproj(x, mods, mod_row, g, wqk, wv, wg, cw, cb, wo=None, wm=None, gn=None, tm=512):
    bsz, s, d = x.shape
    ctx_mode = wo is None
    tm = min(tm, s)
    nh = s // HALO
    r = tm // HALO
    wq = wqk.shape[1]
    in_specs = [pl.BlockSpec((1, tm, d), lambda b, i: (b, i, 0)),
                pl.BlockSpec((1, HALO, d), lambda b, i: (b, jnp.maximum(i * r - 1, 0), 0)),
                pl.BlockSpec((1, HALO, d), lambda b, i: (b, jnp.minimum((i + 1) * r, nh - 1), 0)),
                pl.BlockSpec((1, N_MOD, d), lambda b, i: (mod_row(b), 0, 0)),
                _const_spec((1, d)), _const_spec(wqk.shape), _const_spec(wv.shape),
                _const_spec(wg.shape), _const_spec(cw.shape), _const_spec(cb.shape)]
    args = [x, x, x, mods, g.reshape(1, d), wqk, wv, wg, cw, cb]

    def tok(width, dtype):
        return (pl.BlockSpec((1, tm, width), lambda b, i: (b, i, 0)),
                jax.ShapeDtypeStruct((bsz, s, width), dtype))

    kt = (pl.BlockSpec((1, HEADS, tm // CHUNK, DK, CHUNK), lambda b, i: (b, 0, i, 0, 0)),
          jax.ShapeDtypeStruct((bsz, HEADS, s // CHUNK, DK, CHUNK), BF16))
    zgt = (pl.BlockSpec((1, GATE_ROWS, tm), lambda b, i: (b, 0, i)),
           jax.ShapeDtypeStruct((bsz, GATE_ROWS, s), F32))
    def heads(width):
        return (pl.BlockSpec((1, HEADS, tm, width), lambda b, i: (b, 0, i, 0)),
                jax.ShapeDtypeStruct((bsz, HEADS, s, width), BF16))

    if ctx_mode:
        outs = [kt, heads(DV), zgt]
    else:
        in_specs += [_const_spec(wo.shape), _const_spec(wm.shape), _const_spec(gn.shape)]
        args += [wo, wm, gn]
        outs = [heads(DK), kt, heads(DV), zgt,
                heads(DV), tok(wm.shape[1] // 2, BF16), tok(wm.shape[1] // 2, BF16)]
    return pl.pallas_call(
        functools.partial(_inproj_kernel, ctx_mode=ctx_mode, tm=tm),
        grid=(bsz, s // tm),
        in_specs=in_specs,
        out_specs=[o[0] for o in outs],
        out_shape=[o[1] for o in outs],
        scratch_shapes=[pltpu.VMEM((2, tm + 2 * HALO, COL_TILE), F32), pltpu.VMEM((tm + 2 * HALO, d), BF16),
                        pltpu.VMEM((tm, GATE_PAD), F32)]
        + ([] if ctx_mode else [pltpu.VMEM((tm, wm.shape[1] // 2), F32)]),
        compiler_params=pltpu.CompilerParams(dimension_semantics=("parallel", "parallel"),
                                             vmem_limit_bytes=VMEM_LIMIT),
        name="inproj_ctx" if ctx_mode else "inproj",
    )(*args)


def _lane_scan(x, op, ident, reverse):
    lane = lax.broadcasted_iota(jnp.int32, x.shape, 1)
    n = x.shape[1]
    k = 1
    while k < n:
        if reverse:
            x = op(x, jnp.where(lane < n - k, pltpu.roll(x, n - k, axis=1), ident))
        else:
            x = op(x, jnp.where(lane >= k, pltpu.roll(x, k, axis=1), ident))
        k *= 2
    return x


def _mlstm_kernel(q_ref, kt_ref, v_ref, o_ref, g_ref, ktc_ref, vc_ref, br_ref, ng_ref, out_ref,
                  e_ref, col_ref, tot_ref, emax_ref, rhs_ref, ms_ref, c_ref, m_ref, lhs_ref, floor_ref):
    s = q_ref.shape[2]
    nc = s // CHUNK
    ncc = vc_ref.shape[2] // CHUNK
    row = lax.broadcasted_iota(jnp.int32, (CHUNK, CHUNK), 0)
    col = lax.broadcasted_iota(jnp.int32, (CHUNK, CHUNK), 1)
    eye = row == col
    ones = jnp.ones((CHUNK, NORM_COLS), BF16)
    sel_r = lax.broadcasted_iota(jnp.int32, (3 * CHUNK, 2 * NORM_COLS), 0)
    sel_c = lax.broadcasted_iota(jnp.int32, (3 * CHUNK, 2 * NORM_COLS), 1)
    sel = jnp.where((sel_r < CHUNK) == (sel_c < NORM_COLS), 1.0, 0.0).astype(BF16)

    for d in range(2):
        logi = g_ref[0, 2 * d, 0] + br_ref[0, 2 * d]
        logf = _log_sigmoid(g_ref[0, 2 * d + 1, 0] + br_ref[0, 2 * d + 1])
        b = _lane_scan(logf, jnp.add, 0.0, d == 1)
        e = logi - b
        e_ref[d] = e
        b_hi = b.astype(BF16).astype(F32)
        col_ref[d, 0] = _lane_scan(e, jnp.maximum, -jnp.inf, d == 1).astype(BF16).astype(F32)
        col_ref[d, 1] = b_hi
        col_ref[d, 2] = (b - b_hi).astype(BF16).astype(F32)
        tot_ref[d] = jnp.sum(logf, axis=-1, keepdims=True)
        emax_ref[d] = jnp.max(e, axis=-1, keepdims=True)

    def scan_step(d, r, kt, v):
        e_r = e_ref[d, pl.ds(r, 1), :]
        emax = emax_ref[d, pl.ds(r, 1), :]
        kw = (kt.astype(F32) * jnp.exp(e_r - emax)).astype(BF16)
        v_aug = jnp.concatenate([v, ones], axis=1)
        upd = _dot(kw, v_aug)
        m = m_ref[d]
        c = c_ref[d]
        mx = jnp.maximum(m, emax)
        c_ref[d] = jnp.exp(m - mx) * c + jnp.exp(emax - mx) * upd
        m_ref[d] = tot_ref[d, pl.ds(r, 1), :] + mx
        return c.astype(BF16), m, v_aug

    c_ref[...] = jnp.zeros_like(c_ref)
    m_ref[...] = jnp.zeros_like(m_ref)
    for t in range(ncc):
        for d in range(2):
            ci = t if d == 0 else ncc - 1 - t
            scan_step(d, ci, ktc_ref[0, 0, ci], vc_ref[0, 0, ci * CHUNK:(ci + 1) * CHUNK, :])

    def scan_body(t, carry):
        for d in range(2):
            ci = t if d == 0 else nc - 1 - t
            rows = pl.ds(pl.multiple_of(ci * CHUNK, CHUNK), CHUNK)
            c_in, m_in, v_aug = scan_step(d, CTX_ROWS + ci, kt_ref[0, 0, ci], v_ref[0, 0, rows, :])
            rhs_ref[d, ci, 0:DK, :] = c_in
            rhs_ref[d, ci, DK:, :] = v_aug
            ms_ref[d, ci] = m_in
        return carry

    lax.fori_loop(0, nc, scan_body, 0, unroll=4)

    def weights(ci, slot):
        q = q_ref[0, 0, pl.ds(pl.multiple_of(ci * CHUNK, CHUNK), CHUNK), :]
        qf = q.astype(F32)
        qk = _dot(q, kt_ref[0, 0, ci])
        r = pl.ds(CTX_ROWS + ci, 1)
        for d in range(2):
            inc = (col <= row) if d == 0 else (col >= row)
            e_m = jnp.where(inc, e_ref[d, r, :], -jnp.inf)
            diag = jnp.concatenate([jnp.where(eye, col_ref[d, i, r, :], 0.0) for i in range(3)], axis=1)
            rep = _dot(diag.astype(BF16), sel)
            m = ms_ref[d, ci]
            mm = jnp.maximum(m, rep[:, :NORM_COLS])
            p = (qk * jnp.exp(e_m - mm)).astype(BF16)
            qa = (qf * jnp.exp(m - mm)).astype(BF16)
            lhs_ref[slot, d] = jnp.concatenate([qa, p], axis=1)
            floor_ref[slot, d] = jnp.exp(-(rep[:, NORM_COLS:] + mm))

    def apply(ci, slot):
        rows = pl.ds(pl.multiple_of(ci * CHUNK, CHUNK), CHUNK)
        h = None
        for d in range(2):
            comb = _dot(lhs_ref[slot, d], rhs_ref[d, ci])
            inv = 1.0 / jnp.maximum(jnp.abs(comb[:, DV:]), floor_ref[slot, d])
            hd = comb[:, :DV] * jnp.concatenate([inv] * (DV // NORM_COLS), axis=1)
            h = hd if h is None else h + hd
        out_ref[0, 0, rows, :] = (_rms(h, ng_ref[...]) * o_ref[0, 0, rows, :].astype(F32)).astype(BF16)

    def out_body(ci, carry):
        apply(ci - 1, (ci - 1) % 2)
        weights(jnp.minimum(ci, nc - 1), ci % 2)
        return carry

    weights(0, 0)
    lax.fori_loop(1, nc + 1, out_body, 0, unroll=4)


def _mlstm(q, kt, v, o, g, ktc, vc, bias, norm_g):
    bsz, _, s, _ = q.shape
    sc = vc.shape[2]
    nc, ncc = s // CHUNK, sc // CHUNK
    nr = g.shape[3]
    return pl.pallas_call(
        _mlstm_kernel,
        grid=(bsz, HEADS),
        in_specs=[pl.BlockSpec((1, 1, s, DK), lambda b, h: (b, h, 0, 0)),
                  pl.BlockSpec((1, 1, nc, DK, CHUNK), lambda b, h: (b, h, 0, 0, 0)),
                  pl.BlockSpec((1, 1, s, DV), lambda b, h: (b, h, 0, 0)),
                  pl.BlockSpec((1, 1, s, DV), lambda b, h: (b, h, 0, 0)),
                  pl.BlockSpec((1, 4, 1, nr, CHUNK), lambda b, h: (b, 0, h, 0, 0)),
                  pl.BlockSpec((1, 1, ncc, DK, CHUNK), lambda b, h: (b, h, 0, 0, 0)),
                  pl.BlockSpec((1, 1, sc, DV), lambda b, h: (b, h, 0, 0)),
                  pl.BlockSpec((1, 4, 1, 1), lambda b, h: (h, 0, 0, 0)),
                  pl.BlockSpec((1, DV), lambda b, h: (0, h))],
        out_specs=pl.BlockSpec((1, 1, s, DV), lambda b, h: (b, h, 0, 0)),
        out_shape=jax.ShapeDtypeStruct((bsz, HEADS, s, DV), BF16),
        scratch_shapes=[pltpu.VMEM((2, nr, CHUNK), F32), pltpu.VMEM((2, 3, nr, CHUNK), F32),
                        pltpu.VMEM((2, nr, 1), F32), pltpu.VMEM((2, nr, 1), F32),
                        pltpu.VMEM((2, nc, DK + CHUNK, DV + NORM_COLS), BF16), pltpu.VMEM((2, nc, 1, 1), F32),
                        pltpu.VMEM((2, DK, DV + NORM_COLS), F32), pltpu.VMEM((2, 1, 1), F32),
                        pltpu.VMEM((2, 2, CHUNK, DK + CHUNK), BF16), pltpu.VMEM((2, 2, CHUNK, NORM_COLS), F32)],
        compiler_params=pltpu.CompilerParams(dimension_semantics=("parallel", "parallel"),
                                             vmem_limit_bytes=VMEM_LIMIT),
        name="mlstm",
    )(q, kt, v, o, g, ktc, vc, bias, norm_g)


def _outproj_kernel(x_ref, mod_ref, hx_ref, u_ref, vn_ref, gw_ref, gb_ref, wt_ref, wb_ref, w2i_ref, w2o_ref,
                    o_ref, w2i_bf_ref, w2o_bf_ref, gx_ref, *, cast_steps):
    tm = x_ref.shape[1]

    @pl.when(pl.program_id(0) * pl.num_programs(1) + pl.program_id(1) < cast_steps)
    def _():
        w2i_bf_ref[...] = w2i_ref[...].astype(BF16)
        w2o_bf_ref[...] = w2o_ref[...].astype(BF16)

    for ci in range(tm // CHUNK):
        rows = slice(ci * CHUNK, (ci + 1) * CHUNK)
        for g in range(GROUPS):
            cols = slice(g * GD, (g + 1) * GD)
            sp = _dot(gw_ref[g], vn_ref[0, rows, cols]) + gb_ref[:, cols]
            gx_ref[rows, cols] = (u_ref[0, rows, cols].astype(F32) * sp).astype(BF16)
    y = _dot(gx_ref[...], wb_ref[...])
    for h in range(HEADS):
        y = y + _dot(hx_ref[0, h], wt_ref[h * DV:(h + 1) * DV, :])
    o_ref[0] = x_ref[0] + mod_ref[0, 5:6, :] * y


def _cast_block(extent, max_blocks):
    for blk in range(LANE, extent + 1, LANE):
        if extent % blk == 0 and extent // blk <= max_blocks:
            return blk
    raise ValueError(f"no block of {extent} fits {max_blocks} grid steps")


def _outproj(x, mods, hx, u, vn, gw, gb, wt, wb, w2_in, w2_out, tm=512):
    bsz, s, d = x.shape
    w = u.shape[2]
    steps = bsz * (s // tm)
    cast_cols = _cast_block(w2_in.shape[1], steps)
    cast_rows = _cast_block(w2_out.shape[0], steps)
    ci, ro = w2_in.shape[1] // cast_cols, w2_out.shape[0] // cast_rows
    i_spec = pl.BlockSpec((w2_in.shape[0], cast_cols), lambda b, i: (0, jnp.minimum(b * (s // tm) + i, ci - 1)))
    o_spec = pl.BlockSpec((cast_rows, w2_out.shape[1]), lambda b, i: (jnp.minimum(b * (s // tm) + i, ro - 1), 0))
    return pl.pallas_call(
        functools.partial(_outproj_kernel, cast_steps=max(ci, ro)),
        grid=(bsz, s // tm),
        in_specs=[pl.BlockSpec((1, tm, d), lambda b, i: (b, i, 0)),
                  pl.BlockSpec((1, N_MOD, d), lambda b, i: (b, 0, 0)),
                  pl.BlockSpec((1, HEADS, tm, DV), lambda b, i: (b, 0, i, 0)),
                  pl.BlockSpec((1, tm, w), lambda b, i: (b, i, 0)),
                  pl.BlockSpec((1, tm, w), lambda b, i: (b, i, 0)),
                  _const_spec(gw.shape), _const_spec(gb.shape),
                  _const_spec(wt.shape), _const_spec(wb.shape), i_spec, o_spec],
        out_specs=[pl.BlockSpec((1, tm, d), lambda b, i: (b, i, 0)), i_spec, o_spec],
        out_shape=[jax.ShapeDtypeStruct((bsz, s, d), F32), jax.ShapeDtypeStruct(w2_in.shape, BF16),
                   jax.ShapeDtypeStruct(w2_out.shape, BF16)],
        scratch_shapes=[pltpu.VMEM((tm, w), BF16)],
        compiler_params=pltpu.CompilerParams(dimension_semantics=("arbitrary", "arbitrary"),
                                             vmem_limit_bytes=VMEM_LIMIT),
        name="outproj",
    )(x, mods, hx, u, vn, gw, gb, wt, wb, w2_in, w2_out)


def _gate_rows(zgc, zg):
    def rows(z):
        bsz, _, t = z.shape
        return z.reshape(bsz, 4, HEADS, t // CHUNK, CHUNK)
    gc = rows(zgc)
    assert gc.shape[3] <= CTX_ROWS
    gc = jnp.pad(gc, ((0, 0), (0, 0), (0, 0), (0, CTX_ROWS - gc.shape[3]), (0, 0)))
    return jnp.concatenate([gc, rows(zg)], axis=3)


def kernel(x, c, ctx, c_ctx, w_ada, b_ada, norm_ffn1, w_ffn1_in, w_ffn1_out, norm_mix, w_in, conv_w, conv_b, b_igate, b_fgate, mlstm_norm, gmlp_norm, gmlp_w, gmlp_b, w_out, norm_ffn2, w_ffn2_in, w_ffn2_out, final_norm):
    bsz, s, d = x.shape
    depth = w_ada.shape[0]
    assert depth == 1, "context-stream update for non-final layers is not implemented"
    l = 0
    qk_w = 2 * HEADS * DK
    mw = HEADS * DV
    c_qk, c_v, c_o = qk_w, qk_w + mw, qk_w + 2 * mw
    c_gate = c_o + 4 * HEADS

    lat_row = lambda b: b
    ctx_row = lambda b: bsz

    rows = pl.cdiv(bsz + 1, SUBLANE) * SUBLANE
    cond = jnp.zeros((rows, d), F32).at[:bsz].set(c).at[bsz].set(c_ctx)
    mods = _adaln(cond, w_ada[l], b_ada[l]).reshape(rows, N_MOD, d)

    w1i, w1o = w_ffn1_in[l].astype(BF16), w_ffn1_out[l].astype(BF16)
    wi = w_in[l]
    wqk, wv, wo = (wi[:, a:b].astype(BF16) for a, b in ((0, c_qk), (c_qk, c_v), (c_v, c_o)))
    wg = jnp.pad(wi[:, c_o:c_gate].astype(BF16), ((0, 0), (0, GATE_PAD - 4 * HEADS)))
    wm = wi[:, c_gate:].astype(BF16)
    cw = jnp.pad(conv_w[l], ((0, SUBLANE - CONV_K), (0, 0)))
    cb = conv_b[l].reshape(1, qk_w)
    wout = w_out[l].astype(BF16)

    x = _ffn(x, mods, lat_row, 0, norm_ffn1[l], w1i, w1o)
    ctx = _ffn(ctx.reshape(1, -1, d), mods, ctx_row, 0, norm_ffn1[l], w1i, w1o).reshape(ctx.shape)

    q, kt, v, zg, o, u, vn = _inproj(x, mods, lat_row, norm_mix[l], wqk, wv, wg, cw, cb,
                                     wo, wm, gmlp_norm[l].reshape(1, -1))
    ktc, vc, zgc = _inproj(ctx, mods, ctx_row, norm_mix[l], wqk[:, qk_w // 2:], wv, wg,
                           cw[:, qk_w // 2:], cb[:, qk_w // 2:])

    bias = jnp.stack([b_igate[l][0], b_fgate[l][0], b_igate[l][1], b_fgate[l][1]], axis=-1)
    hx = _mlstm(q, kt, v, o, _gate_rows(zgc, zg), ktc, vc, bias.reshape(HEADS, 4, 1, 1),
                mlstm_norm[l].reshape(1, mw))

    gb = jnp.repeat(gmlp_b[l].T, GD, axis=1)
    x, w2i, w2o = _outproj(x, mods, hx, u, vn, gmlp_w[l].astype(BF16), gb, wout[:mw], wout[mw:],
                           w_ffn2_in[l], w_ffn2_out[l])

    return _ffn(x, mods, lat_row, 6, norm_ffn2[l], w2i, w2o, final_g=final_norm)
```

```python
import functools

import jax
import jax.numpy as jnp
from jax import lax
from jax.experimental import pallas as pl
from jax.experimental.pallas import tpu as pltpu

F32 = jnp.float32
BF16 = jnp.bfloat16

EPS = 1e-6
N_MOD = 9
HEADS = 4
DK = 128
DV = 256
CHUNK = 128
GROUPS = 8
GD = 128
CONV_K = 5

LANE = 128
SUBLANE = 8
MXU_WIDTH = 256
VMEM_LIMIT = 56 * 1024 * 1024

HALO = 2 * SUBLANE
ROW_SLAB = 2 * SUBLANE
COL_TILE = MXU_WIDTH
LOOKAHEAD = 1
GATE_PAD = LANE
GATE_ROWS = 4 * HEADS
CTX_ROWS = SUBLANE
NORM_COLS = LANE
SCAN_UNROLL = 16
assert CHUNK == DK == NORM_COLS == LANE and CONV_K // 2 <= HALO


def _sigmoid(x):
    return 1.0 / (1.0 + jnp.exp(-x))


def _log_sigmoid(x):
    return jnp.minimum(x, 0.0) - jnp.log(1.0 + jnp.exp(-jnp.abs(x)))


def _gelu_tanh(x):
    c2 = -2.0 * 0.7978845608028654
    return x / (1.0 + jnp.exp(x * (c2 + (c2 * 0.044715) * (x * x))))


def _rms(x, g):
    return x * lax.rsqrt(jnp.mean(x * x, axis=-1, keepdims=True) + EPS) * g


def _dot(a, b):
    return jnp.dot(a, b, preferred_element_type=F32)


def _ada_kernel(c_ref, w_ref, b_ref, o_ref):
    c = c_ref[...]
    s = (c * _sigmoid(c)).astype(BF16)
    o_ref[...] = _dot(s, w_ref[...].astype(BF16)) + b_ref[...]


def _adaln(cond, w, b, tn=1024):
    rows, d = cond.shape
    n = w.shape[1]
    return pl.pallas_call(
        _ada_kernel,
        grid=(n // tn,),
        in_specs=[pl.BlockSpec((rows, d), lambda j: (0, 0)),
                  pl.BlockSpec((d, tn), lambda j: (0, j)),
                  pl.BlockSpec((1, tn), lambda j: (0, j))],
        out_specs=pl.BlockSpec((rows, tn), lambda j: (0, j)),
        out_shape=jax.ShapeDtypeStruct((rows, n), F32),
        compiler_params=pltpu.CompilerParams(dimension_semantics=("parallel",),
                                             vmem_limit_bytes=VMEM_LIMIT),
        name="adaln",
    )(cond, w, b.reshape(1, n))


def _ffn_kernel(*refs, mod_base, final):
    if final:
        x_ref, mod_ref, g_ref, wg_ref, wu_ref, wo_ref, fn_ref, o_ref, xn_ref, r_ref = refs
    else:
        x_ref, mod_ref, g_ref, wg_ref, wu_ref, wo_ref, o_ref, xn_ref = refs
    f = pl.program_id(2)
    tm = xn_ref.shape[0]

    @pl.when(f == 0)
    def _():
        shift = mod_ref[0, mod_base:mod_base + 1, :]
        gain = g_ref[...] * (1.0 + mod_ref[0, mod_base + 1:mod_base + 2, :])

        def slab(i, carry):
            rows = pl.ds(pl.multiple_of(i * ROW_SLAB, ROW_SLAB), ROW_SLAB)
            x = x_ref[0, rows, :]
            r = lax.rsqrt(jnp.mean(x * x, axis=-1, keepdims=True) + EPS)
            xn_ref[rows, :] = (x * r * gain + shift).astype(BF16)
            o_ref[0, rows, :] = jnp.zeros_like(x)
            return carry

        lax.fori_loop(0, tm // ROW_SLAB, slab, 0, unroll=8)

    xn = xn_ref[...]
    gate = _dot(xn, wg_ref[...])
    up = _dot(xn, wu_ref[...])
    h = (gate * _sigmoid(gate) * up).astype(BF16)
    o_ref[0] += _dot(h, wo_ref[...])

    @pl.when(f == pl.num_programs(2) - 1)
    def _():
        half_gate = 0.5 * mod_ref[0, mod_base + 2:mod_base + 3, :]

        def slab(i, carry):
            rows = pl.ds(pl.multiple_of(i * ROW_SLAB, ROW_SLAB), ROW_SLAB)
            y = x_ref[0, rows, :] + half_gate * o_ref[0, rows, :]
            o_ref[0, rows, :] = y
            if final:
                r_ref[rows, :] = lax.rsqrt(jnp.mean(y * y, axis=-1, keepdims=True) + EPS)
            return carry

        lax.fori_loop(0, tm // ROW_SLAB, slab, 0, unroll=8)

        if final:
            def scale(i, carry):
                rows = pl.ds(pl.multiple_of(i * ROW_SLAB, ROW_SLAB), ROW_SLAB)
                o_ref[0, rows, :] = o_ref[0, rows, :] * r_ref[rows, :] * fn_ref[...]
                return carry

            lax.fori_loop(0, tm // ROW_SLAB, scale, 0, unroll=8)


def _ffn(x, mods, mod_row, mod_base, g, w_in, w_out, final_g=None, tm=1024, tf=512):
    bsz, s, d = x.shape
    ff = w_out.shape[0]
    tm = min(tm, s)
    nf = ff // tf
    final = final_g is not None
    in_specs = [pl.BlockSpec((1, tm, d), lambda b, i, f: (b, i, 0)),
                pl.BlockSpec((1, N_MOD, d), lambda b, i, f: (mod_row(b), 0, 0)),
                pl.BlockSpec((1, d), lambda b, i, f: (0, 0)),
                pl.BlockSpec((d, tf), lambda b, i, f: (0, f)),
                pl.BlockSpec((d, tf), lambda b, i, f: (0, nf + f)),
                pl.BlockSpec((tf, d), lambda b, i, f: (f, 0))]
    args = [x, mods, g.reshape(1, d), w_in, w_in, w_out]
    if final:
        in_specs.append(pl.BlockSpec((1, d), lambda b, i, f: (0, 0)))
        args.append(final_g.reshape(1, d))
    return pl.pallas_call(
        functools.partial(_ffn_kernel, mod_base=mod_base, final=final),
        grid=(bsz, s // tm, nf),
        in_specs=in_specs,
        out_specs=pl.BlockSpec((1, tm, d), lambda b, i, f: (b, i, 0)),
        out_shape=jax.ShapeDtypeStruct((bsz, s, d), F32),
        scratch_shapes=[pltpu.VMEM((tm, d), BF16)] + ([pltpu.VMEM((tm, 1), F32)] if final else []),
        compiler_params=pltpu.CompilerParams(
            dimension_semantics=("parallel", "parallel", "arbitrary"),
            vmem_limit_bytes=VMEM_LIMIT),
        name="ffn_final" if final else "ffn",
    )(*args)


def _inproj_kernel(*refs, ctx_mode, tm):
    if ctx_mode:
        (x_ref, xp_ref, xq_ref, mod_ref, g_ref, wqk_ref, wv_ref, wg_ref, cw_ref, cb_ref,
         k_ref, v_ref, zg_ref, zs_ref, xn_ref, zgs_ref) = refs
    else:
        (x_ref, xp_ref, xq_ref, mod_ref, g_ref, wqk_ref, wv_ref, wg_ref, cw_ref, cb_ref,
         wo_ref, wm_ref, gn_ref,
         q_ref, k_ref, v_ref, zg_ref, o_ref, u_ref, vn_ref, zs_ref, xn_ref, zgs_ref, gv_ref) = refs
    i = pl.program_id(1)
    g = g_ref[...]
    shift = mod_ref[0, 3:4, :]
    gain = g * (1.0 + mod_ref[0, 4:5, :])

    def norm_rows(x):
        r = lax.rsqrt(jnp.mean(x * x, axis=-1, keepdims=True) + EPS)
        return (x * r * gain + shift).astype(BF16)

    xn_ref[0:HALO, :] = norm_rows(xp_ref[0])
    xn_ref[HALO + tm:, :] = norm_rows(xq_ref[0])

    def slab(t, carry):
        rows = pl.ds(pl.multiple_of(t * ROW_SLAB, ROW_SLAB), ROW_SLAB)
        xn_ref[pl.ds(pl.multiple_of(HALO + t * ROW_SLAB, ROW_SLAB), ROW_SLAB), :] = norm_rows(x_ref[0, rows, :])
        return carry

    lax.fori_loop(0, tm // ROW_SLAB, slab, 0, unroll=8)
    xn = xn_ref[HALO:HALO + tm, :]

    def col(t):
        return slice(t * COL_TILE, (t + 1) * COL_TILE)

    tasks = []
    ssq = [jnp.zeros((tm, 1), F32)]

    if not ctx_mode:
        gw = wm_ref.shape[1] // 2

        def gv_epilogue(z, t):
            gv = _gelu_tanh(z)
            gv_ref[:, col(t)] = gv
            ssq[0] = ssq[0] + jnp.sum(gv * gv, axis=-1, keepdims=True)

        def u_epilogue(z, t):
            u_ref[0, :, col(t)] = _gelu_tanh(z).astype(BF16)

        for t in range(gw // COL_TILE):
            tasks.append((functools.partial(lambda t: _dot(xn, wm_ref[:, gw + t * COL_TILE:gw + (t + 1) * COL_TILE]), t),
                          functools.partial(gv_epilogue, t=t)))
        for t in range(gw // COL_TILE):
            tasks.append((functools.partial(lambda t: _dot(xn, wm_ref[:, col(t)]), t),
                          functools.partial(u_epilogue, t=t)))

    wq = wqk_ref.shape[1]
    k_off = 0 if ctx_mode else wq // 2

    def qk_epilogue(z, t):
        zs = zs_ref.at[t % 2]
        zs[...] = z
        zs[0:HALO, :] = jnp.where(i == 0, 0.0, z[0:HALO])
        zs[HALO + tm:, :] = jnp.where(i == pl.num_programs(1) - 1, 0.0, z[HALO + tm:])
        conv = cb_ref[:, col(t)]
        for j in range(CONV_K):
            off = HALO - CONV_K // 2 + j
            conv = conv + zs[off:off + tm, :] * cw_ref[j:j + 1, col(t)]
        qk = conv * _sigmoid(conv)
        for hh in range(COL_TILE // DK):
            c0 = t * COL_TILE + hh * DK
            if c0 < k_off:
                q_ref[0, c0 // DK] = (qk[:, hh * DK:(hh + 1) * DK] * (DK ** -0.5)).astype(BF16)
            else:
                for cc in range(tm // CHUNK):
                    k_ref[0, (c0 - k_off) // DK, cc] = (
                        qk[cc * CHUNK:(cc + 1) * CHUNK, hh * DK:(hh + 1) * DK].T.astype(BF16))

    for t in range(wq // COL_TILE):
        tasks.append((functools.partial(lambda t: _dot(xn_ref[...], wqk_ref[:, col(t)]), t),
                      functools.partial(qk_epilogue, t=t)))

    if not ctx_mode:
        def o_epilogue(z, h):
            o_ref[0, h] = _sigmoid(z).astype(BF16)

        for h in range(HEADS):
            tasks.append((functools.partial(lambda h: _dot(xn, wo_ref[:, h * DV:(h + 1) * DV]), h),
                          functools.partial(o_epilogue, h=h)))

    def v_epilogue(z, h):
        if h == 0 and not ctx_mode:
            r = lax.rsqrt(ssq[0] * (1.0 / gw) + EPS)
            vn_ref[0] = (gv_ref[...] * r * gn_ref[...]).astype(BF16)
        v_ref[0, h] = z.astype(BF16)

    for h in range(HEADS):
        tasks.append((functools.partial(lambda h: _dot(xn, wv_ref[:, h * DV:(h + 1) * DV]), h),
                      functools.partial(v_epilogue, h=h)))

    def gate_epilogue(z):
        zgs_ref[...] = z
        zg_ref[0] = zgs_ref[...].T[:GATE_ROWS, :]

    tasks.append((lambda: _dot(xn, wg_ref[...]), gate_epilogue))

    if not ctx_mode:
        n_gm, n_qk = 2 * (gw // COL_TILE), wq // COL_TILE
        heavy = list(range(n_gm + n_qk))
        light = list(range(n_gm + n_qk, len(tasks)))
        lead = gw // COL_TILE - 1
        order = heavy[:lead]
        for a, b in zip(heavy[lead:], light):
            order += [a, b]
        order += heavy[lead + len(light):] + light[len(heavy) - lead:]
        assert sorted(order) == list(range(len(tasks)))
        tasks = [tasks[k] for k in order]

    pending =[product() for product, _ in tasks[:LOOKAHEAD]]
    for n, (_, epilogue) in enumerate(tasks):
        if n + LOOKAHEAD < len(tasks):
            pending.append(tasks[n + LOOKAHEAD][0]())
        epilogue(pending.pop(0))


def _const_spec(shape):
    nd = len(shape)
    return pl.BlockSpec(shape, lambda b, i: (0,) * nd, pipeline_mode=pl.Buffered(1))


def _inproj(x, mods, mod_row, g, wqk, wv, wg, cw, cb, wo=None, wm=None, gn=None, tm=512):
    bsz, s, d = x.shape
    ctx_mode = wo is None
    tm = min(tm, s)
    nh = s // HALO
    r = tm // HALO
    wq = wqk.shape[1]
    in_specs = [pl.BlockSpec((1, tm, d), lambda b, i: (b, i, 0)),
                pl.BlockSpec((1, HALO, d), lambda b, i: (b, jnp.maximum(i * r - 1, 0), 0)),
                pl.BlockSpec((1, HALO, d), lambda b, i: (b, jnp.minimum((i + 1) * r, nh - 1), 0)),
                pl.BlockSpec((1, N_MOD, d), lambda b, i: (mod_row(b), 0, 0)),
                _const_spec((1, d)), _const_spec(wqk.shape), _const_spec(wv.shape),
                _const_spec(wg.shape), _const_spec(cw.shape), _const_spec(cb.shape)]
    args = [x, x, x, mods, g.reshape(1, d), wqk, wv, wg, cw, cb]

    def tok(width, dtype):
        return (pl.BlockSpec((1, tm, width), lambda b, i: (b, i, 0)),
                jax.ShapeDtypeStruct((bsz, s, width), dtype))

    kt = (pl.BlockSpec((1, HEADS, tm // CHUNK, DK, CHUNK), lambda b, i: (b, 0, i, 0, 0)),
          jax.ShapeDtypeStruct((bsz, HEADS, s // CHUNK, DK, CHUNK), BF16))
    zgt = (pl.BlockSpec((1, GATE_ROWS, tm), lambda b, i: (b, 0, i)),
           jax.ShapeDtypeStruct((bsz, GATE_ROWS, s), F32))
    def heads(width):
        return (pl.BlockSpec((1, HEADS, tm, width), lambda b, i: (b, 0, i, 0)),
                jax.ShapeDtypeStruct((bsz, HEADS, s, width), BF16))

    if ctx_mode:
        outs = [kt, heads(DV), zgt]
    else:
        in_specs += [_const_spec(wo.shape), _const_spec(wm.shape), _const_spec(gn.shape)]
        args += [wo, wm, gn]
        outs = [heads(DK), kt, heads(DV), zgt,
                heads(DV), tok(wm.shape[1] // 2, BF16), tok(wm.shape[1] // 2, BF16)]
    return pl.pallas_call(
        functools.partial(_inproj_kernel, ctx_mode=ctx_mode, tm=tm),
        grid=(bsz, s // tm),
        in_specs=in_specs,
        out_specs=[o[0] for o in outs],
        out_shape=[o[1] for o in outs],
        scratch_shapes=[pltpu.VMEM((2, tm + 2 * HALO, COL_TILE), F32), pltpu.VMEM((tm + 2 * HALO, d), BF16),
                        pltpu.VMEM((tm, GATE_PAD), F32)]
        + ([] if ctx_mode else [pltpu.VMEM((tm, wm.shape[1] // 2), F32)]),
        compiler_params=pltpu.CompilerParams(dimension_semantics=("parallel", "parallel"),
                                             vmem_limit_bytes=VMEM_LIMIT),
        name="inproj_ctx" if ctx_mode else "inproj",
    )(*args)


def _lane_scan(x, op, ident, reverse):
    lane = lax.broadcasted_iota(jnp.int32, x.shape, 1)
    n = x.shape[1]
    k = 1
    while k < n:
        if reverse:
            x = op(x, jnp.where(lane < n - k, pltpu.roll(x, n - k, axis=1), ident))
        else:
            x = op(x, jnp.where(lane >= k, pltpu.roll(x, k, axis=1), ident))
        k *= 2
    return x


def _mlstm_kernel(q_ref, kt_ref, v_ref, o_ref, g_ref, ktc_ref, vc_ref, br_ref, ng_ref, out_ref,
                  e_ref, col_ref, tot_ref, emax_ref, rhs_ref, ms_ref, c_ref, m_ref, lhs_ref, floor_ref):
    s = q_ref.shape[2]
    nc = s // CHUNK
    ncc = vc_ref.shape[2] // CHUNK
    row = lax.broadcasted_iota(jnp.int32, (CHUNK, CHUNK), 0)
    col = lax.broadcasted_iota(jnp.int32, (CHUNK, CHUNK), 1)
    eye = row == col
    ones = jnp.ones((CHUNK, NORM_COLS), BF16)
    sel_r = lax.broadcasted_iota(jnp.int32, (3 * CHUNK, 2 * NORM_COLS), 0)
    sel_c = lax.broadcasted_iota(jnp.int32, (3 * CHUNK, 2 * NORM_COLS), 1)
    sel = jnp.where((sel_r < CHUNK) == (sel_c < NORM_COLS), 1.0, 0.0).astype(BF16)

    for d in range(2):
        logi = g_ref[0, 2 * d, 0] + br_ref[0, 2 * d]
        logf = _log_sigmoid(g_ref[0, 2 * d + 1, 0] + br_ref[0, 2 * d + 1])
        b = _lane_scan(logf, jnp.add, 0.0, d == 1)
        e = logi - b
        e_ref[d] = e
        b_hi = b.astype(BF16).astype(F32)
        col_ref[d, 0] = _lane_scan(e, jnp.maximum, -jnp.inf, d == 1).astype(BF16).astype(F32)
        col_ref[d, 1] = b_hi
        col_ref[d, 2] = (b - b_hi).astype(BF16).astype(F32)
        tot_ref[d] = jnp.sum(logf, axis=-1, keepdims=True)
        emax_ref[d] = jnp.max(e, axis=-1, keepdims=True)

    def scan_step(d, r, kt, v):
        e_r = e_ref[d, pl.ds(r, 1), :]
        emax = emax_ref[d, pl.ds(r, 1), :]
        kw = (kt.astype(F32) * jnp.exp(e_r - emax)).astype(BF16)
        v_aug = jnp.concatenate([v, ones], axis=1)
        upd = _dot(kw, v_aug)
        m = m_ref[d]
        c = c_ref[d]
        mx = jnp.maximum(m, emax)
        c_ref[d] = jnp.exp(m - mx) * c + jnp.exp(emax - mx) * upd
        m_ref[d] = tot_ref[d, pl.ds(r, 1), :] + mx
        return c.astype(BF16), m, v_aug

    c_ref[...] = jnp.zeros_like(c_ref)
    m_ref[...] = jnp.zeros_like(m_ref)
    for t in range(ncc):
        for d in range(2):
            ci = t if d == 0 else ncc - 1 - t
            scan_step(d, ci, ktc_ref[0, 0, ci], vc_ref[0, 0, ci * CHUNK:(ci + 1) * CHUNK, :])

    def scan_body(t, carry):
        for d in range(2):
            ci = t if d == 0 else nc - 1 - t
            rows = pl.ds(pl.multiple_of(ci * CHUNK, CHUNK), CHUNK)
            c_in, m_in, v_aug = scan_step(d, CTX_ROWS + ci, kt_ref[0, 0, ci], v_ref[0, 0, rows, :])
            rhs_ref[d, ci, 0:DK, :] = c_in
            rhs_ref[d, ci, DK:, :] = v_aug
            ms_ref[d, ci] = m_in
        return carry

    lax.fori_loop(0, nc, scan_body, 0, unroll=min(SCAN_UNROLL, nc))

    def weights(ci, slot):
        q = q_ref[0, 0, pl.ds(pl.multiple_of(ci * CHUNK, CHUNK), CHUNK), :]
        qf = q.astype(F32)
        qk = _dot(q, kt_ref[0, 0, ci])
        r = pl.ds(CTX_ROWS + ci, 1)
        for d in range(2):
            inc = (col <= row) if d == 0 else (col >= row)
            e_m = jnp.where(inc, e_ref[d, r, :], -jnp.inf)
            diag = jnp.concatenate([jnp.where(eye, col_ref[d, i, r, :], 0.0) for i in range(3)], axis=1)
            rep = _dot(diag.astype(BF16), sel)
            m = ms_ref[d, ci]
            mm = jnp.maximum(m, rep[:, :NORM_COLS])
            p = (qk * jnp.exp(e_m - mm)).astype(BF16)
            qa = (qf * jnp.exp(m - mm)).astype(BF16)
            lhs_ref[slot, d] = jnp.concatenate([qa, p], axis=1)
            floor_ref[slot, d] = jnp.exp(-(rep[:, NORM_COLS:] + mm))

    def apply(ci, slot):
        rows = pl.ds(pl.multiple_of(ci * CHUNK, CHUNK), CHUNK)
        h = None
        for d in range(2):
            comb = _dot(lhs_ref[slot, d], rhs_ref[d, ci])
            inv = 1.0 / jnp.maximum(jnp.abs(comb[:, DV:]), floor_ref[slot, d])
            hd = comb[:, :DV] * jnp.concatenate([inv] * (DV // NORM_COLS), axis=1)
            h = hd if h is None else h + hd
        out_ref[0, 0, rows, :] = (_rms(h, ng_ref[...]) * o_ref[0, 0, rows, :].astype(F32)).astype(BF16)

    def out_body(ci, carry):
        apply(ci - 1, (ci - 1) % 2)
        weights(jnp.minimum(ci, nc - 1), ci % 2)
        return carry

    weights(0, 0)
    lax.fori_loop(1, nc + 1, out_body, 0, unroll=nc)


def _mlstm(q, kt, v, o, g, ktc, vc, bias, norm_g):
    bsz, _, s, _ = q.shape
    sc = vc.shape[2]
    nc, ncc = s // CHUNK, sc // CHUNK
    nr = g.shape[3]
    return pl.pallas_call(
        _mlstm_kernel,
        grid=(bsz, HEADS),
        in_specs=[pl.BlockSpec((1, 1, s, DK), lambda b, h: (b, h, 0, 0)),
                  pl.BlockSpec((1, 1, nc, DK, CHUNK), lambda b, h: (b, h, 0, 0, 0)),
                  pl.BlockSpec((1, 1, s, DV), lambda b, h: (b, h, 0, 0)),
                  pl.BlockSpec((1, 1, s, DV), lambda b, h: (b, h, 0, 0)),
                  pl.BlockSpec((1, 4, 1, nr, CHUNK), lambda b, h: (b, 0, h, 0, 0)),
                  pl.BlockSpec((1, 1, ncc, DK, CHUNK), lambda b, h: (b, h, 0, 0, 0)),
                  pl.BlockSpec((1, 1, sc, DV), lambda b, h: (b, h, 0, 0)),
                  pl.BlockSpec((1, 4, 1, 1), lambda b, h: (h, 0, 0, 0)),
                  pl.BlockSpec((1, DV), lambda b, h: (0, h))],
        out_specs=pl.BlockSpec((1, 1, s, DV), lambda b, h: (b, h, 0, 0)),
        out_shape=jax.ShapeDtypeStruct((bsz, HEADS, s, DV), BF16),
        scratch_shapes=[pltpu.VMEM((2, nr, CHUNK), F32), pltpu.VMEM((2, 3, nr, CHUNK), F32),
                        pltpu.VMEM((2, nr, 1), F32), pltpu.VMEM((2, nr, 1), F32),
                        pltpu.VMEM((2, nc, DK + CHUNK, DV + NORM_COLS), BF16), pltpu.VMEM((2, nc, 1, 1), F32),
                        pltpu.VMEM((2, DK, DV + NORM_COLS), F32), pltpu.VMEM((2, 1, 1), F32),
                        pltpu.VMEM((2, 2, CHUNK, DK + CHUNK), BF16), pltpu.VMEM((2, 2, CHUNK, NORM_COLS), F32)],
        compiler_params=pltpu.CompilerParams(dimension_semantics=("parallel", "parallel"),
                                             vmem_limit_bytes=VMEM_LIMIT),
        name="mlstm",
    )(q, kt, v, o, g, ktc, vc, bias, norm_g)


def _outproj_kernel(x_ref, mod_ref, hx_ref, u_ref, vn_ref, gw_ref, gb_ref, wt_ref, wb_ref, w2i_ref, w2o_ref,
                    o_ref, w2i_bf_ref, w2o_bf_ref, gx_ref, *, cast_steps):
    tm = x_ref.shape[1]

    @pl.when(pl.program_id(0) * pl.num_programs(1) + pl.program_id(1) < cast_steps)
    def _():
        w2i_bf_ref[...] = w2i_ref[...].astype(BF16)
        w2o_bf_ref[...] = w2o_ref[...].astype(BF16)

    for ci in range(tm // CHUNK):
        rows = slice(ci * CHUNK, (ci + 1) * CHUNK)
        for g in range(GROUPS):
            cols = slice(g * GD, (g + 1) * GD)
            sp = _dot(gw_ref[g], vn_ref[0, rows, cols]) + gb_ref[:, cols]
            gx_ref[rows, cols] = (u_ref[0, rows, cols].astype(F32) * sp).astype(BF16)
    y = _dot(gx_ref[...], wb_ref[...])
    for h in range(HEADS):
        y = y + _dot(hx_ref[0, h], wt_ref[h * DV:(h + 1) * DV, :])
    o_ref[0] = x_ref[0] + mod_ref[0, 5:6, :] * y


def _cast_block(extent, max_blocks):
    for blk in range(LANE, extent + 1, LANE):
        if extent % blk == 0 and extent // blk <= max_blocks:
            return blk
    raise ValueError(f"no block of {extent} fits {max_blocks} grid steps")


def _outproj(x, mods, hx, u, vn, gw, gb, wt, wb, w2_in, w2_out, tm=512):
    bsz, s, d = x.shape
    w = u.shape[2]
    steps = bsz * (s // tm)
    cast_cols = _cast_block(w2_in.shape[1], steps)
    cast_rows = _cast_block(w2_out.shape[0], steps)
    ci, ro = w2_in.shape[1] // cast_cols, w2_out.shape[0] // cast_rows
    i_spec = pl.BlockSpec((w2_in.shape[0], cast_cols), lambda b, i: (0, jnp.minimum(b * (s // tm) + i, ci - 1)))
    o_spec = pl.BlockSpec((cast_rows, w2_out.shape[1]), lambda b, i: (jnp.minimum(b * (s // tm) + i, ro - 1), 0))
    return pl.pallas_call(
        functools.partial(_outproj_kernel, cast_steps=max(ci, ro)),
        grid=(bsz, s // tm),
        in_specs=[pl.BlockSpec((1, tm, d), lambda b, i: (b, i, 0)),
                  pl.BlockSpec((1, N_MOD, d), lambda b, i: (b, 0, 0)),
                  pl.BlockSpec((1, HEADS, tm, DV), lambda b, i: (b, 0, i, 0)),
                  pl.BlockSpec((1, tm, w), lambda b, i: (b, i, 0)),
                  pl.BlockSpec((1, tm, w), lambda b, i: (b, i, 0)),
                  _const_spec(gw.shape), _const_spec(gb.shape),
                  _const_spec(wt.shape), _const_spec(wb.shape), i_spec, o_spec],
        out_specs=[pl.BlockSpec((1, tm, d), lambda b, i: (b, i, 0)), i_spec, o_spec],
        out_shape=[jax.ShapeDtypeStruct((bsz, s, d), F32), jax.ShapeDtypeStruct(w2_in.shape, BF16),
                   jax.ShapeDtypeStruct(w2_out.shape, BF16)],
        scratch_shapes=[pltpu.VMEM((tm, w), BF16)],
        compiler_params=pltpu.CompilerParams(dimension_semantics=("arbitrary", "arbitrary"),
                                             vmem_limit_bytes=VMEM_LIMIT),
        name="outproj",
    )(x, mods, hx, u, vn, gw, gb, wt, wb, w2_in, w2_out)


def _gate_rows(zgc, zg):
    def rows(z):
        bsz, _, t = z.shape
        return z.reshape(bsz, 4, HEADS, t // CHUNK, CHUNK)
    gc = rows(zgc)
    assert gc.shape[3] <= CTX_ROWS
    gc = jnp.pad(gc, ((0, 0), (0, 0), (0, 0), (0, CTX_ROWS - gc.shape[3]), (0, 0)))
    return jnp.concatenate([gc, rows(zg)], axis=3)


def kernel(x, c, ctx, c_ctx, w_ada, b_ada, norm_ffn1, w_ffn1_in, w_ffn1_out, norm_mix, w_in, conv_w, conv_b, b_igate, b_fgate, mlstm_norm, gmlp_norm, gmlp_w, gmlp_b, w_out, norm_ffn2, w_ffn2_in, w_ffn2_out, final_norm):
    bsz, s, d = x.shape
    depth = w_ada.shape[0]
    assert depth == 1, "context-stream update for non-final layers is not implemented"
    l = 0
    qk_w = 2 * HEADS * DK
    mw = HEADS * DV
    c_qk, c_v, c_o = qk_w, qk_w + mw, qk_w + 2 * mw
    c_gate = c_o + 4 * HEADS

    lat_row = lambda b: b
    ctx_row = lambda b: bsz

    rows = pl.cdiv(bsz + 1, SUBLANE) * SUBLANE
    cond = jnp.zeros((rows, d), F32).at[:bsz].set(c).at[bsz].set(c_ctx)
    mods = _adaln(cond, w_ada[l], b_ada[l]).reshape(rows, N_MOD, d)

    w1i, w1o = w_ffn1_in[l].astype(BF16), w_ffn1_out[l].astype(BF16)
    wi = w_in[l]
    wqk, wv, wo = (wi[:, a:b].astype(BF16) for a, b in ((0, c_qk), (c_qk, c_v), (c_v, c_o)))
    wg = jnp.pad(wi[:, c_o:c_gate].astype(BF16), ((0, 0), (0, GATE_PAD - 4 * HEADS)))
    wm = wi[:, c_gate:].astype(BF16)
    cw = jnp.pad(conv_w[l], ((0, SUBLANE - CONV_K), (0, 0)))
    cb = conv_b[l].reshape(1, qk_w)
    wout = w_out[l].astype(BF16)

    x = _ffn(x, mods, lat_row, 0, norm_ffn1[l], w1i, w1o)
    ctx = _ffn(ctx.reshape(1, -1, d), mods, ctx_row, 0, norm_ffn1[l], w1i, w1o).reshape(ctx.shape)

    q, kt, v, zg, o, u, vn = _inproj(x, mods, lat_row, norm_mix[l], wqk, wv, wg, cw, cb,
                                     wo, wm, gmlp_norm[l].reshape(1, -1))
    ktc, vc, zgc = _inproj(ctx, mods, ctx_row, norm_mix[l], wqk[:, qk_w // 2:], wv, wg,
                           cw[:, qk_w // 2:], cb[:, qk_w // 2:])

    bias = jnp.stack([b_igate[l][0], b_fgate[l][0], b_igate[l][1], b_fgate[l][1]], axis=-1)
    hx = _mlstm(q, kt, v, o, _gate_rows(zgc, zg), ktc, vc, bias.reshape(HEADS, 4, 1, 1),
                mlstm_norm[l].reshape(1, mw))

    gb = jnp.repeat(gmlp_b[l].T, GD, axis=1)
    x, w2i, w2o = _outproj(x, mods, hx, u, vn, gmlp_w[l].astype(BF16), gb, wout[:mw], wout[mw:],
                           w_ffn2_in[l], w_ffn2_out[l])

    return _ffn(x, mods, lat_row, 6, norm_ffn2[l], w2i, w2o, final_g=final_norm)
```
